```python
import math
import jax, jax.numpy as jnp
from jax import lax
import numpy as np

D_MODEL = 2048
BATCH = 4
SEQ = 2048
DEPTH = 2
DEC_BATCH = 32
DEC_SEQ = 8
PAST_LEN = 8192
PAGE_SIZE = 128

HEAD_DIM = 64
N_ATT_HEADS = 16
N_RWKV_HEADS = 16
C_ATT = N_ATT_HEADS * HEAD_DIM
C_RWKV = N_RWKV_HEADS * HEAD_DIM
MOBA_BLOCK = 256
MOBA_TOPK = 3
MOBA_Q_CHUNK = 64
D_DECAY_LORA = max(32, int(round(1.8 * C_RWKV ** 0.5 / 32)) * 32)
D_AAA_LORA = max(32, int(round(1.8 * C_RWKV ** 0.5 / 32)) * 32)
D_MV_LORA = max(32, int(round(1.3 * C_RWKV ** 0.5 / 32)) * 32)
D_GATE_LORA = max(32, int(round(0.6 * C_RWKV ** 0.8 / 32)) * 32)
C_SHIFT = 3 * C_RWKV + D_DECAY_LORA + D_AAA_LORA + D_GATE_LORA
C_IN = 3 * C_ATT + C_SHIFT
RW_SPLITS = (C_RWKV, 2 * C_RWKV, 3 * C_RWKV, 3 * C_RWKV + D_DECAY_LORA,
             3 * C_RWKV + D_DECAY_LORA + D_AAA_LORA)
LN_X_EPS = 64e-5
RMS_EPS = 1e-5
N_EXPERTS = 32
TOP_K = 4
D_EXPERT = D_MODEL
SWIGLU_LIMIT = 7.0
SWIGLU_ALPHA = 1.702

kernel_name = "hymba_moba_rwkv7_moe_adaln_step"


def _rms_norm(x, g):
    xf = x.astype(jnp.float32)
    y = xf * lax.rsqrt(jnp.mean(xf * xf, axis=-1, keepdims=True) + RMS_EPS)
    return (y * g.astype(jnp.float32)).astype(x.dtype)


def _moba_attention(q, k, v, q_pos, q_chunk):
    B, Sq, H, hd = q.shape
    L = k.shape[1]
    n_full = L // MOBA_BLOCK
    n_blk = -(-L // MOBA_BLOCK)
    pad = n_blk * MOBA_BLOCK - L
    kb = jnp.pad(k, ((0, 0), (0, pad), (0, 0), (0, 0))).reshape(B, n_blk, MOBA_BLOCK, H, hd)
    vb = jnp.pad(v, ((0, 0), (0, pad), (0, 0), (0, 0))).reshape(B, n_blk, MOBA_BLOCK, H, hd)
    kmean = jnp.mean(kb[:, :n_full].astype(jnp.float32), axis=2)
    topk = min(MOBA_TOPK, n_full)
    n_chunks = Sq // q_chunk
    qc = q.reshape(B * n_chunks, q_chunk, H, hd)
    pc = jnp.broadcast_to(q_pos.reshape(1, n_chunks, q_chunk), (B, n_chunks, q_chunk)).reshape(B * n_chunks, q_chunk)
    bidx = jnp.repeat(jnp.arange(B, dtype=jnp.int32), n_chunks)
    scale = HEAD_DIM ** -0.5
    hsel = jnp.arange(H)[None, :, None, None]
    rsel = jnp.arange(MOBA_BLOCK)[None, None, None, :]

    def one(args):
        qq, pos, b = args
        kbb = kb[b]
        vbb = vb[b]
        own = pos // MOBA_BLOCK
        own_idx = jnp.broadcast_to(own[:, None, None], (q_chunk, H, 1))
        if topk > 0:
            s = jnp.einsum('qhd,nhd->qhn', qq.astype(jnp.float32), kmean[b])
            past = jnp.arange(n_full)[None, None, :] < own[:, None, None]
            s = jnp.where(past, s, -jnp.inf)
            vals, sel = lax.top_k(s, topk)
            idx = jnp.concatenate([sel.astype(jnp.int32), own_idx.astype(jnp.int32)], axis=-1)
            valid = jnp.concatenate([jnp.isfinite(vals), jnp.ones(own_idx.shape, bool)], axis=-1)
        else:
            idx = own_idx.astype(jnp.int32)
            valid = jnp.ones(own_idx.shape, bool)
        kg = kbb[idx[..., None], rsel, hsel]
        vg = vbb[idx[..., None], rsel, hsel]
        logits = jnp.einsum('qhd,qhsrd->qhsr', qq, kg).astype(jnp.float32) * scale
        key_pos = idx[..., None] * MOBA_BLOCK + rsel
        mask = valid[..., None] & (key_pos <= pos[:, None, None, None])
        logits = jnp.where(mask, logits, -jnp.inf)
        Qn, Hn, Sn, R = logits.shape
        p = jax.nn.softmax(logits.reshape(Qn, Hn, Sn * R), axis=-1).reshape(Qn, Hn, Sn, R)
        return jnp.einsum('qhsr,qhsrd->qhd', p.astype(vg.dtype), vg)

    out = lax.map(one, (qc, pc, bidx))
    return out.reshape(B, Sq, H, hd)


def _rwkv7_time_mix(pr, shift0, S0, v_first, mu, w0, w_up, a0, a_up, g_up, k_k, k_a, r_k, ln_w, ln_b, vres):
    B, T, _ = pr.shape
    f32 = jnp.float32
    prev = jnp.concatenate([shift0[:, None, :].astype(pr.dtype), pr[:, :-1]], axis=1)
    xm = (pr + (prev - pr) * mu).astype(f32)
    r, k, v, dw, da, dg = jnp.split(xm, RW_SPLITS, axis=-1)
    w_log = -jax.nn.softplus(-(w0 + jnp.tanh(dw) @ w_up)) - 0.5
    decay = jnp.exp(-jnp.exp(w_log))
    a = jax.nn.sigmoid(a0 + da @ a_up)
    g = jax.nn.sigmoid(dg) @ g_up
    if vres is None:
        v_first = v
    else:
        v0, v_dn, v_upm = vres
        v = v + (v_first - v) * jax.nn.sigmoid(v0 + (v @ v_dn) @ v_upm)
    heads = lambda t: t.reshape(B, T, N_RWKV_HEADS, HEAD_DIM)
    kk = heads(k * k_k)
    kk = kk * lax.rsqrt(jnp.maximum(jnp.sum(kk * kk, axis=-1, keepdims=True), 1e-24))
    k = k * (1.0 + (a - 1.0) * k_a)
    r, k, v, a, decay = heads(r), heads(k), heads(v), heads(a), heads(decay)

    def step(S, inp):
        r_t, w_t, k_t, v_t, kk_t, a_t = inp
        sa = jnp.einsum('bhvk,bhk->bhv', S, -kk_t)
        S = S * w_t[:, :, None, :] + sa[..., None] * (kk_t * a_t)[:, :, None, :] \
            + v_t[..., None] * k_t[:, :, None, :]
        return S, jnp.einsum('bhvk,bhk->bhv', S, r_t)

    xs = jnp.moveaxis(jnp.stack([r, decay, k, v, kk, a]), 2, 0)
    S_T, ys = lax.scan(step, S0.astype(f32), xs)
    y = jnp.moveaxis(ys, 0, 1)
    mean = jnp.mean(y, axis=-1, keepdims=True)
    var = jnp.mean(jnp.square(y - mean), axis=-1, keepdims=True)
    y = ((y - mean) * lax.rsqrt(var + LN_X_EPS)).reshape(B, T, C_RWKV) * ln_w + ln_b
    y = heads(y) + jnp.sum(r * k * r_k, axis=-1, keepdims=True) * v
    y = y.reshape(B, T, C_RWKV) * g
    return y.astype(pr.dtype), S_T, v_first


def _moe(h, w_router, b_router, w1, b1, w2, b2):
    B, T, D = h.shape
    hf = h.reshape(B * T, D)
    logits = (hf @ w_router + b_router).astype(jnp.float32)
    vals, idx = lax.top_k(logits, TOP_K)
    wts = jax.nn.softmax(vals, axis=-1)
    gates = jnp.sum(jax.nn.one_hot(idx, N_EXPERTS, dtype=jnp.float32) * wts[..., None], axis=1)

    def expert(acc, ep):
        w1e, b1e, w2e, b2e, ge = ep
        hh = hf @ w1e + b1e
        glu = jnp.minimum(hh[:, 0::2], SWIGLU_LIMIT)
        lin = jnp.clip(hh[:, 1::2], -SWIGLU_LIMIT, SWIGLU_LIMIT)
        act = glu * jax.nn.sigmoid(SWIGLU_ALPHA * glu) * (lin + 1.0)
        out = act @ w2e + b2e
        return acc + ge[:, None] * out.astype(jnp.float32), None

    acc, _ = lax.scan(expert, jnp.zeros((B * T, D), jnp.float32), (w1, b1, w2, b2, gates.T))
    return acc.astype(h.dtype).reshape(B, T, D)


def _trunk(x, c, q_pos, q_chunk, shift0, wkv0, P, cache_k=None, cache_v=None, page_table=None):
    B, T, _ = x.shape
    v_first = None
    new_k, new_v, new_shift, new_wkv = [], [], [], []
    for l in range(DEPTH):
        mod = (c @ P['w_ada'][l] + P['b_ada'][l])[:, None, :]
        sh1, sc1, gt1, sh2, sc2, gt2 = jnp.split(mod, 6, axis=-1)
        h = _rms_norm(x, P['norm_mix'][l]) * (1.0 + sc1) + sh1
        proj = h @ P['w_in'][l]
        q, k, v, pr = jnp.split(proj, [C_ATT, 2 * C_ATT, 3 * C_ATT], axis=-1)
        q = q.reshape(B, T, N_ATT_HEADS, HEAD_DIM)
        k = k.reshape(B, T, N_ATT_HEADS, HEAD_DIM)
        v = v.reshape(B, T, N_ATT_HEADS, HEAD_DIM)
        if cache_k is None:
            k_all, v_all = k, v
        else:
            k_past = cache_k[l, page_table].reshape(B, -1, N_ATT_HEADS, HEAD_DIM)
            v_past = cache_v[l, page_table].reshape(B, -1, N_ATT_HEADS, HEAD_DIM)
            k_all = jnp.concatenate([k_past.astype(k.dtype), k], axis=1)
            v_all = jnp.concatenate([v_past.astype(v.dtype), v], axis=1)
        att = _moba_attention(q, k_all, v_all, q_pos, q_chunk).reshape(B, T, C_ATT)
        vres = None if l == 0 else (P['rw_v0'][l - 1], P['rw_v_down'][l - 1], P['rw_v_up'][l - 1])
        rw, S_T, v_first = _rwkv7_time_mix(
            pr, shift0[l], wkv0[l], v_first, P['rw_mu'][l], P['rw_w0'][l], P['rw_w_up'][l],
            P['rw_a0'][l], P['rw_a_up'][l], P['rw_g_up'][l], P['rw_k_k'][l], P['rw_k_a'][l],
            P['rw_r_k'][l], P['rw_ln_w'][l], P['rw_ln_b'][l], vres)
        mix = jnp.concatenate([att, rw.astype(att.dtype)], axis=-1) @ P['w_out'][l]
        x = x + gt1 * mix
        h2 = _rms_norm(x, P['norm_ffn'][l]) * (1.0 + sc2) + sh2
        x = x + gt2 * _moe(h2, P['w_router'][l], P['b_router'][l], P['w_mlp1'][l],
                           P['b_mlp1'][l], P['w_mlp2'][l], P['b_mlp2'][l])
        new_k.append(k)
        new_v.append(v)
        new_shift.append(pr[:, -1])
        new_wkv.append(S_T)
    y = _rms_norm(x, P['norm_final'])
    return y, jnp.stack(new_k), jnp.stack(new_v), jnp.stack(new_shift), jnp.stack(new_wkv)


def setup_inputs(seed: int = 0) -> dict:
    key = jax.random.key(seed)
    ks = iter(jax.random.split(key, 40))
    f32 = jnp.float32
    nrm = lambda shape, s: s * jax.random.normal(next(ks), shape, f32)
    uni = lambda shape, lo, hi: jax.random.uniform(next(ks), shape, f32, lo, hi)
    n_pages = PAST_LEN // PAGE_SIZE
    n_used = DEC_BATCH * n_pages
    n_pool = n_used + max(1, n_used // 4)
    d = {}
    d['x_prompt'] = nrm((BATCH, SEQ, D_MODEL), 1.0)
    d['x_sample'] = nrm((DEC_BATCH, DEC_SEQ, D_MODEL), 1.0)
    d['cache_k'] = nrm((DEPTH, n_pool, PAGE_SIZE, N_ATT_HEADS, HEAD_DIM), 1.0)
    d['cache_v'] = nrm((DEPTH, n_pool, PAGE_SIZE, N_ATT_HEADS, HEAD_DIM), 1.0)
    d['page_table'] = jax.random.permutation(next(ks), n_pool)[:n_used].reshape(DEC_BATCH, n_pages).astype(jnp.int32)
    d['state_shift'] = nrm((DEPTH, DEC_BATCH, C_SHIFT), 1.0)
    d['state_wkv'] = nrm((DEPTH, DEC_BATCH, N_RWKV_HEADS, HEAD_DIM, HEAD_DIM), 0.5)
    d['c_prompt'] = nrm((BATCH, D_MODEL), 1.0)
    d['c_sample'] = nrm((DEC_BATCH, D_MODEL), 1.0)
    d['w_ada'] = nrm((DEPTH, D_MODEL, 6 * D_MODEL), 0.2 * D_MODEL ** -0.5)
    d['b_ada'] = nrm((DEPTH, 6 * D_MODEL), 0.02)
    d['norm_mix'] = 1.0 + nrm((DEPTH, D_MODEL), 0.02)
    d['norm_ffn'] = 1.0 + nrm((DEPTH, D_MODEL), 0.02)
    d['w_in'] = nrm((DEPTH, D_MODEL, C_IN), D_MODEL ** -0.5)
    d['w_out'] = nrm((DEPTH, C_ATT + C_RWKV, D_MODEL), (C_ATT + C_RWKV) ** -0.5)
    d['rw_mu'] = uni((DEPTH, C_SHIFT), 0.0, 1.0)
    d['rw_w0'] = uni((DEPTH, C_RWKV), -5.0, 0.5)
    d['rw_w_up'] = nrm((DEPTH, D_DECAY_LORA, C_RWKV), 0.1)
    d['rw_a0'] = nrm((DEPTH, C_RWKV), 0.5)
    d['rw_a_up'] = nrm((DEPTH, D_AAA_LORA, C_RWKV), 0.1)
    d['rw_g_up'] = nrm((DEPTH, D_GATE_LORA, C_RWKV), D_GATE_LORA ** -0.5)
    d['rw_k_k'] = 0.85 + nrm((DEPTH, C_RWKV), 0.02)
    d['rw_k_a'] = 1.0 + nrm((DEPTH, C_RWKV), 0.02)
    d['rw_r_k'] = nrm((DEPTH, N_RWKV_HEADS, HEAD_DIM), 0.1)
    d['rw_ln_w'] = 1.0 + nrm((DEPTH, C_RWKV), 0.02)
    d['rw_ln_b'] = nrm((DEPTH, C_RWKV), 0.02)
    d['rw_v0'] = 1.0 + nrm((DEPTH - 1, C_RWKV), 0.1)
    d['rw_v_down'] = nrm((DEPTH - 1, C_RWKV, D_MV_LORA), C_RWKV ** -0.5)
    d['rw_v_up'] = nrm((DEPTH - 1, D_MV_LORA, C_RWKV), 0.1)
    d['w_router'] = nrm((DEPTH, D_MODEL, N_EXPERTS), D_MODEL ** -0.5)
    d['b_router'] = nrm((DEPTH, N_EXPERTS), 0.01)
    d['w_mlp1'] = nrm((DEPTH, N_EXPERTS, D_MODEL, 2 * D_EXPERT), D_MODEL ** -0.5)
    d['b_mlp1'] = nrm((DEPTH, N_EXPERTS, 2 * D_EXPERT), 0.01)
    d['w_mlp2'] = nrm((DEPTH, N_EXPERTS, D_EXPERT, D_MODEL), D_EXPERT ** -0.5)
    d['b_mlp2'] = nrm((DEPTH, N_EXPERTS, D_MODEL), 0.01)
    d['norm_final'] = 1.0 + nrm((D_MODEL,), 0.02)
    return d


def reference(x_prompt, x_sample, cache_k, cache_v, page_table, state_shift, state_wkv, c_prompt, c_sample,
              w_ada, b_ada, norm_mix, norm_ffn, w_in, w_out, rw_mu, rw_w0, rw_w_up, rw_a0, rw_a_up,
              rw_g_up, rw_k_k, rw_k_a, rw_r_k, rw_ln_w, rw_ln_b, rw_v0, rw_v_down, rw_v_up,
              w_router, b_router, w_mlp1, b_mlp1, w_mlp2, b_mlp2, norm_final):
    P = dict(w_ada=w_ada, b_ada=b_ada, norm_mix=norm_mix, norm_ffn=norm_ffn, w_in=w_in, w_out=w_out,
             rw_mu=rw_mu, rw_w0=rw_w0, rw_w_up=rw_w_up, rw_a0=rw_a0, rw_a_up=rw_a_up, rw_g_up=rw_g_up,
             rw_k_k=rw_k_k, rw_k_a=rw_k_a, rw_r_k=rw_r_k, rw_ln_w=rw_ln_w, rw_ln_b=rw_ln_b,
             rw_v0=rw_v0, rw_v_down=rw_v_down, rw_v_up=rw_v_up, w_router=w_router, b_router=b_router,
             w_mlp1=w_mlp1, b_mlp1=b_mlp1, w_mlp2=w_mlp2, b_mlp2=b_mlp2, norm_final=norm_final)
    Bp, Tp, _ = x_prompt.shape
    Bs, Ts, _ = x_sample.shape
    shift0 = jnp.zeros((DEPTH, Bp, C_SHIFT), x_prompt.dtype)
    wkv0 = jnp.zeros((DEPTH, Bp, N_RWKV_HEADS, HEAD_DIM, HEAD_DIM), jnp.float32)
    pos_p = jnp.arange(Tp, dtype=jnp.int32)
    y_prompt, k_p, v_p, shift_p, wkv_p = _trunk(
        x_prompt, c_prompt, pos_p, math.gcd(MOBA_Q_CHUNK, Tp), shift0, wkv0, P)
    pos_s = PAST_LEN + jnp.arange(Ts, dtype=jnp.int32)
    y_sample, k_s, v_s, shift_s, wkv_s = _trunk(
        x_sample, c_sample, pos_s, Ts, state_shift, state_wkv, P, cache_k, cache_v, page_table)
    k_p_pages = k_p.reshape(DEPTH, Bp, Tp // PAGE_SIZE, PAGE_SIZE, N_ATT_HEADS, HEAD_DIM)
    v_p_pages = v_p.reshape(DEPTH, Bp, Tp // PAGE_SIZE, PAGE_SIZE, N_ATT_HEADS, HEAD_DIM)
    return (y_prompt, y_sample, k_p_pages, v_p_pages, shift_p, wkv_p, k_s, v_s, shift_s, wkv_s)
```

```python
import functools

import jax
import jax.numpy as jnp
from jax import lax
from jax.experimental import pallas as pl
from jax.experimental.pallas import tpu as pltpu

F32, BF16, I32 = jnp.float32, jnp.bfloat16, jnp.int32

LANES = 128
SUBLANES = 8
VMEM_LIMIT = 50 * 1024 * 1024

D_MODEL = 2048
HEAD_DIM = 64
N_HEADS = 16
C_HEADS = N_HEADS * HEAD_DIM
N_PAIRS = C_HEADS // LANES
MOBA_BLOCK = 256
MOBA_TOPK = 3
PAGE_SIZE = 128
D_DECAY_LORA, D_AAA_LORA, D_GATE_LORA, D_MV_LORA = 64, 64, 160, 32
C_SHIFT = 3 * C_HEADS + D_DECAY_LORA + D_AAA_LORA + D_GATE_LORA
C_SHIFT_PAD = 3584
C_LORA_PAD = C_SHIFT_PAD - 3 * C_HEADS
LN_X_EPS = 64e-5
RMS_EPS = 1e-5
N_EXPERTS = 32
TOP_K = 4
D_EXPERT = D_MODEL
SWIGLU_LIMIT = 7.0
SWIGLU_ALPHA = 1.702
NEG_BIG = -3.0e38

ROW_TILE = 256
SEQ_TABLE_ROWS = 128
EXPERT_TILE = 512
EXPERT_FBLOCK = 256


def _cparams(n_axes):
    return pltpu.CompilerParams(dimension_semantics=("arbitrary",) * n_axes,
                                vmem_limit_bytes=VMEM_LIMIT)


def _dot(a, b):
    return jnp.dot(a, b, preferred_element_type=F32)


def _dot_nt(a, b):
    return lax.dot_general(a, b, (((1,), (1,)), ((), ())), preferred_element_type=F32)


def _split2(x):
    hi = x.astype(BF16)
    lo = (x - hi.astype(F32)).astype(BF16)
    return hi, lo


def _split3(x):
    hi = x.astype(BF16)
    r = x - hi.astype(F32)
    mid = r.astype(BF16)
    lo = (r - mid.astype(F32)).astype(BF16)
    return hi, mid, lo


def _dot3(a, b, nt=False):
    d = _dot_nt if nt else _dot
    ah, al = _split2(a)
    bh, bl = _split2(b)
    return d(ah, bh) + (d(ah, bl) + d(al, bh))


def _dot_exact_lhs(a_bf16, b):
    bh, bm, bl = _split3(b)
    return _dot(a_bf16, bh) + (_dot(a_bf16, bm) + _dot(a_bf16, bl))


def _seq_onehot(row0, rows, grp):
    n_prompt_rows, tp_shift, ts_shift, n_prompt = grp
    r = row0 + lax.broadcasted_iota(I32, (rows, SEQ_TABLE_ROWS), 0)
    lane = lax.broadcasted_iota(I32, (rows, SEQ_TABLE_ROWS), 1)
    sid = jnp.where(r < n_prompt_rows, r >> tp_shift, n_prompt + ((r - n_prompt_rows) >> ts_shift))
    return jnp.where(lane == sid, 1.0, 0.0).astype(BF16)


def _head_segsum(x, e_ref):
    e = e_ref[...]
    outs = []
    for c in range(x.shape[1] // LANES):
        hi, lo = _split2(x[:, c * LANES:(c + 1) * LANES])
        outs.append(_dot(hi, e) + _dot(lo, e))
    return jnp.concatenate(outs, axis=1)


def _sigmoid(x):
    return 1.0 / (1.0 + jnp.exp(-x))


def _ada_kernel(c_ref, w_ref, b_ref, o_ref):
    o_ref[0] = _dot3(c_ref[...], w_ref[0]) + b_ref[0]


def _ada(c_pad, w_ada, b_ada):
    depth, d, n = w_ada.shape
    tn = 512
    return pl.pallas_call(
        _ada_kernel,
        out_shape=jax.ShapeDtypeStruct((depth, SEQ_TABLE_ROWS, n), F32),
        grid=(depth, n // tn),
        in_specs=[pl.BlockSpec((SEQ_TABLE_ROWS, d), lambda l, j: (0, 0)),
                  pl.BlockSpec((1, d, tn), lambda l, j: (l, 0, j)),
                  pl.BlockSpec((1, 1, tn), lambda l, j: (l, 0, j))],
        out_specs=pl.BlockSpec((1, SEQ_TABLE_ROWS, tn), lambda l, j: (l, 0, j)),
        compiler_params=_cparams(2), name="ada_mod",
    )(c_pad, w_ada, b_ada.reshape(depth, 1, n))


def _rms_modulated(x_ref, g_ref, sc_ref, sh_ref, grp):
    rows = x_ref.shape[0]
    x = x_ref[...]
    y = x * lax.rsqrt(jnp.mean(x * x, axis=-1, keepdims=True) + RMS_EPS) * g_ref[...]
    oh = _seq_onehot(pl.program_id(0) * rows, rows, grp)
    sc = _dot_exact_lhs(oh, sc_ref[...])
    sh = _dot_exact_lhs(oh, sh_ref[...])
    return y * (1.0 + sc) + sh


def _rms_mod_kernel(x_ref, g_ref, sc_ref, sh_ref, o_ref, *, grp):
    o_ref[...] = _rms_modulated(x_ref, g_ref, sc_ref, sh_ref, grp).astype(o_ref.dtype)


def _mod_spec(k):
    return pl.BlockSpec((SEQ_TABLE_ROWS, D_MODEL), lambda i, k=k: (0, k))


def _rms_mod(x, g, mod, k_sc, k_sh, grp):
    t = x.shape[0]
    return pl.pallas_call(
        functools.partial(_rms_mod_kernel, grp=grp),
        out_shape=jax.ShapeDtypeStruct((t, D_MODEL), BF16),
        grid=(t // ROW_TILE,),
        in_specs=[pl.BlockSpec((ROW_TILE, D_MODEL), lambda i: (i, 0)),
                  pl.BlockSpec((1, D_MODEL), lambda i: (0, 0)),
                  _mod_spec(k_sc), _mod_spec(k_sh)],
        out_specs=pl.BlockSpec((ROW_TILE, D_MODEL), lambda i: (i, 0)),
        compiler_params=_cparams(1), name="rms_mod",
    )(x, g.reshape(1, D_MODEL), mod, mod)


def _rms_router_kernel(x_ref, g_ref, sc_ref, sh_ref, wr_ref, br_ref, h_ref, route_ref, cnt_ref,
                       carry, *, grp):
    i = pl.program_id(0)
    rows = x_ref.shape[0]
    h = _rms_modulated(x_ref, g_ref, sc_ref, sh_ref, grp)
    h_ref[...] = h
    logits = _dot3(h, wr_ref[...]) + br_ref[...]
    lane = lax.broadcasted_iota(I32, (rows, LANES), 1)
    vals, idxs = [], []
    multi = jnp.zeros((rows, LANES), F32)
    for _ in range(TOP_K):
        m = jnp.max(logits, axis=1, keepdims=True)
        sel = jnp.min(jnp.where(logits == m, lane, LANES), axis=1, keepdims=True)
        hit = lane == sel
        vals.append(m)
        idxs.append(sel)
        multi = jnp.where(hit, 1.0, multi)
        logits = jnp.where(hit, NEG_BIG, logits)
    es = [jnp.exp(v - vals[0]) for v in vals]
    denom = es[0] + es[1] + es[2] + es[3]

    @pl.when(i == 0)
    def _():
        carry[...] = jnp.zeros_like(carry)

    r_i = lax.broadcasted_iota(I32, (rows, rows), 0)
    c_i = lax.broadcasted_iota(I32, (rows, rows), 1)
    tri = jnp.where(c_i < r_i, 1.0, 0.0).astype(BF16)
    cum = _dot(tri, multi.astype(BF16)) + carry[0:1, :]
    route = jnp.zeros((rows, LANES), F32)
    for j in range(TOP_K):
        pos = jnp.sum(jnp.where(lane == idxs[j], cum, 0.0), axis=1, keepdims=True)
        route = jnp.where(lane == j, idxs[j].astype(F32), route)
        route = jnp.where(lane == TOP_K + j, es[j] / denom, route)
        route = jnp.where(lane == 2 * TOP_K + j, pos, route)
    route_ref[...] = route
    new_cnt = carry[0:1, :] + jnp.sum(multi, axis=0, keepdims=True)
    carry[...] = jnp.broadcast_to(new_cnt, carry.shape)
    cnt_ref[...] = jnp.broadcast_to(new_cnt, cnt_ref.shape)


def _rms_router(x, g, mod, k_sc, k_sh, wr_pad, br_pad, grp):
    t = x.shape[0]
    return pl.pallas_call(
        functools.partial(_rms_router_kernel, grp=grp),
        out_shape=(jax.ShapeDtypeStruct((t, D_MODEL), F32),
                   jax.ShapeDtypeStruct((t, LANES), F32),
                   jax.ShapeDtypeStruct((SUBLANES, LANES), F32)),
        grid=(t // ROW_TILE,),
        in_specs=[pl.BlockSpec((ROW_TILE, D_MODEL), lambda i: (i, 0)),
                  pl.BlockSpec((1, D_MODEL), lambda i: (0, 0)),
                  _mod_spec(k_sc), _mod_spec(k_sh),
                  pl.BlockSpec((D_MODEL, LANES), lambda i: (0, 0)),
                  pl.BlockSpec((1, LANES), lambda i: (0, 0))],
        out_specs=(pl.BlockSpec((ROW_TILE, D_MODEL), lambda i: (i, 0)),
                   pl.BlockSpec((ROW_TILE, LANES), lambda i: (i, 0)),
                   pl.BlockSpec((SUBLANES, LANES), lambda i: (0, 0))),
        scratch_shapes=[pltpu.VMEM((SUBLANES, LANES), F32)],
        compiler_params=_cparams(1), name="rms_router",
    )(x, g.reshape(1, D_MODEL), mod, mod, wr_pad, br_pad)


def _final_norm_kernel(x_ref, g_ref, o_ref):
    x = x_ref[...]
    o_ref[...] = x * lax.rsqrt(jnp.mean(x * x, axis=-1, keepdims=True) + RMS_EPS) * g_ref[...]


def _final_norm(x, g):
    t = x.shape[0]
    return pl.pallas_call(
        _final_norm_kernel,
        out_shape=jax.ShapeDtypeStruct((t, D_MODEL), F32),
        grid=(t // ROW_TILE,),
        in_specs=[pl.BlockSpec((ROW_TILE, D_MODEL), lambda i: (i, 0)),
                  pl.BlockSpec((1, D_MODEL), lambda i: (0, 0))],
        out_specs=pl.BlockSpec((ROW_TILE, D_MODEL), lambda i: (i, 0)),
        compiler_params=_cparams(1), name="final_norm",
    )(x, g.reshape(1, D_MODEL))


def _mm_kernel(a_ref, w_ref, o_ref):
    o_ref[...] = _dot(a_ref[...], w_ref[...])


def _matmul(a, w, tm, tn):
    m, k = a.shape
    n = w.shape[1]
    return pl.pallas_call(
        _mm_kernel,
        out_shape=jax.ShapeDtypeStruct((m, n), F32),
        grid=(n // tn, m // tm),
        in_specs=[pl.BlockSpec((tm, k), lambda j, i: (i, 0)),
                  pl.BlockSpec((k, tn), lambda j, i: (0, j))],
        out_specs=pl.BlockSpec((tm, tn), lambda j, i: (i, j)),
        compiler_params=_cparams(2), name="proj_matmul",
    )(a, w)


def _mix_out_kernel(att_ref, rw_ref, wa_ref, wb_ref, x_ref, gt_ref, o_ref, *, grp):
    rows = x_ref.shape[0]
    mix = _dot(att_ref[...], wa_ref[...]) + _dot(rw_ref[...], wb_ref[...])
    oh = _seq_onehot(pl.program_id(1) * rows, rows, grp)
    o_ref[...] = x_ref[...] + _dot_exact_lhs(oh, gt_ref[...]) * mix


def _mix_out(att, rw, w_out, x, mod, k_gt, grp):
    t = x.shape[0]
    tm, tn = ROW_TILE, 1024
    nb = D_MODEL // tn
    return pl.pallas_call(
        functools.partial(_mix_out_kernel, grp=grp),
        out_shape=jax.ShapeDtypeStruct((t, D_MODEL), F32),
        grid=(nb, t // tm),
        in_specs=[pl.BlockSpec((tm, C_HEADS), lambda j, i: (i, 0)),
                  pl.BlockSpec((tm, C_HEADS), lambda j, i: (i, 0)),
                  pl.BlockSpec((C_HEADS, tn), lambda j, i: (0, j)),
                  pl.BlockSpec((C_HEADS, tn), lambda j, i: (1, j)),
                  pl.BlockSpec((tm, tn), lambda j, i: (i, j)),
                  pl.BlockSpec((SEQ_TABLE_ROWS, tn), lambda j, i, k=k_gt, nb=nb: (0, k * nb + j))],
        out_specs=pl.BlockSpec((tm, tn), lambda j, i: (i, j)),
        compiler_params=_cparams(2), name="mix_out",
    )(att, rw, w_out, w_out, x, mod)


def _pattn_kernel(q_ref, k_ref, v_ref, o_ref, kmean):
    blk = q_ref.shape[0]
    n_blk = k_ref.shape[0] // blk
    qi = pl.program_id(2)
    lane = lax.broadcasted_iota(I32, (blk, LANES), 1)
    low = lane < HEAD_DIM

    @pl.when(qi == 0)
    def _():
        rows = [jnp.sum(k_ref[n * blk:(n + 1) * blk, :], axis=0, keepdims=True) * (1.0 / blk)
                for n in range(n_blk)]
        rows.append(jnp.zeros((LANES - n_blk, LANES), F32))
        kmean[...] = jnp.concatenate(rows, axis=0)

    q = q_ref[...] * (HEAD_DIM ** -0.5)
    qh = (jnp.where(low, q, 0.0), jnp.where(low, 0.0, q))
    km = kmean[...]
    selm = []
    for h in range(2):
        s = _dot3(qh[h], km, nt=True)
        cnt = jnp.zeros((blk, LANES), F32)
        for m in range(n_blk):
            sm = s[:, m:m + 1]
            beats = (sm > s) | ((sm == s) & (m < lane))
            cnt = cnt + jnp.where(beats, 1.0, 0.0) * jnp.where(m < qi, 1.0, 0.0)
        selm.append(jnp.where((cnt < MOBA_TOPK) & (lane < qi), 1.0, 0.0))
    qb = (qh[0].astype(BF16), qh[1].astype(BF16))
    row_i = lax.broadcasted_iota(I32, (blk, blk), 0)
    col_i = lax.broadcasted_iota(I32, (blk, blk), 1)
    causal = jnp.where(col_i <= row_i, 1.0, 0.0)

    def body(n, carry):
        off = pl.multiple_of(n * blk, blk)
        kb = k_ref[pl.ds(off, blk), :].astype(BF16)
        vb = v_ref[pl.ds(off, blk), :].astype(BF16)
        out = []
        for h in range(2):
            m_old, l_old, acc = carry[3 * h:3 * h + 3]
            logits = _dot_nt(qb[h], kb)
            flag = jnp.sum(jnp.where(lane == n, selm[h], 0.0), axis=1, keepdims=True)
            own = jnp.where(n == qi, 1.0, 0.0)
            allowed = (own * causal + (1.0 - own) * flag) > 0.5
            masked = jnp.where(allowed, logits, NEG_BIG)
            m_new = jnp.maximum(m_old, jnp.max(masked, axis=1, keepdims=True))
            alpha = jnp.exp(m_old - m_new)
            p = jnp.where(allowed, jnp.exp(masked - m_new), 0.0)
            l_new = l_old * alpha + jnp.sum(p, axis=1, keepdims=True)
            acc = acc * alpha + _dot(p.astype(BF16), vb)
            out += [m_new, l_new, acc]
        return tuple(out)

    init = (jnp.full((blk, 1), NEG_BIG, F32), jnp.zeros((blk, 1), F32), jnp.zeros((blk, LANES), F32)) * 2
    res = lax.fori_loop(0, qi + 1, body, init)
    o_ref[...] = jnp.where(low, res[2] / res[1], res[5] / res[4]).astype(o_ref.dtype)


def _prompt_attention(qkv, n_seq, seq_len):
    blk = MOBA_BLOCK
    nq = seq_len // blk
    return pl.pallas_call(
        _pattn_kernel,
        out_shape=jax.ShapeDtypeStruct((n_seq * seq_len, C_HEADS), BF16),
        grid=(n_seq, N_PAIRS, nq),
        in_specs=[pl.BlockSpec((blk, LANES), lambda b, p, i: (b * nq + i, p)),
                  pl.BlockSpec((seq_len, LANES), lambda b, p, i: (b, N_PAIRS + p)),
                  pl.BlockSpec((seq_len, LANES), lambda b, p, i: (b, 2 * N_PAIRS + p))],
        out_specs=pl.BlockSpec((blk, LANES), lambda b, p, i: (b * nq + i, p)),
        scratch_shapes=[pltpu.VMEM((LANES, LANES), F32)],
        compiler_params=_cparams(3), name="moba_prompt",
    )(qkv, qkv, qkv)


def _sattn_k_kernel(pt_ref, qr_ref, kp_ref, lg_ref, sc_ref, ksum):
    j = pl.program_id(1)
    kt = kp_ref[0]
    qr = qr_ref[0]
    lg_ref[0, 0] = _dot(qr.astype(BF16), kt.astype(BF16))
    pages_per_block = MOBA_BLOCK // PAGE_SIZE

    @pl.when(j == 0)
    def _():
        sc_ref[0] = jnp.zeros(sc_ref.shape[1:], F32)

    @pl.when(j % pages_per_block == 0)
    def _():
        ksum[...] = kt

    @pl.when(j % pages_per_block == pages_per_block - 1)
    def _():
        g = _dot3(qr, ksum[...] + kt)
        s = jnp.sum(g, axis=1, keepdims=True) * (1.0 / MOBA_BLOCK)
        lane = lax.broadcasted_iota(I32, g.shape, 1)
        sc_ref[0] = jnp.where(lane == j // pages_per_block, s, sc_ref[0])


def _sattn_v_kernel(pt_ref, lg_ref, sc_ref, vp_ref, qr_ref, kn_ref, vn_ref, o_ref,
                    selm, m_s, l_s, acc, *, n_new):
    j = pl.program_id(1)
    n_pages = pl.num_programs(1)
    nr = lg_ref.shape[2]
    lane = lax.broadcasted_iota(I32, (nr, LANES), 1)
    n_blocks = (n_pages * PAGE_SIZE) // MOBA_BLOCK
    pages_per_block = MOBA_BLOCK // PAGE_SIZE

    @pl.when(j == 0)
    def _():
        s = sc_ref[0]
        cnt = jnp.zeros((nr, LANES), F32)
        for m in range(n_blocks):
            sm = s[:, m:m + 1]
            cnt = cnt + jnp.where((sm > s) | ((sm == s) & (m < lane)), 1.0, 0.0)
        selm[...] = jnp.where((cnt < MOBA_TOPK) & (lane < n_blocks), 1.0, 0.0)
        m_s[...] = jnp.full(m_s.shape, NEG_BIG, F32)
        l_s[...] = jnp.zeros(l_s.shape, F32)
        acc[...] = jnp.zeros(acc.shape, F32)

    def accumulate(logits, allowed, v_nt=None, v_nn=None):
        masked = jnp.where(allowed, logits, NEG_BIG)
        m_old = m_s[...]
        m_new = jnp.maximum(m_old, jnp.max(masked, axis=1, keepdims=True))
        alpha = jnp.exp(m_old - m_new)
        p = jnp.where(allowed, jnp.exp(masked - m_new), 0.0)
        l_s[...] = l_s[...] * alpha + jnp.sum(p, axis=1, keepdims=True)
        pv = _dot_nt(p.astype(BF16), v_nt) if v_nn is None else _dot(p.astype(BF16), v_nn)
        acc[...] = acc[...] * alpha + pv
        m_s[...] = m_new

    flag = jnp.sum(jnp.where(lane == j // pages_per_block, selm[...], 0.0), axis=1, keepdims=True)

    @pl.when(jnp.max(flag) > 0.0)
    def _():
        accumulate(lg_ref[0, 0], jnp.broadcast_to(flag, (nr, LANES)) > 0.0, v_nt=vp_ref[0].astype(BF16))

    @pl.when(j == n_pages - 1)
    def _():
        pad = jnp.zeros((LANES - n_new, C_HEADS), F32)
        kn = jnp.concatenate([kn_ref[...], pad], axis=0).astype(BF16)
        vn = jnp.concatenate([vn_ref[...], pad], axis=0).astype(BF16)
        logits = _dot_nt(qr_ref[0].astype(BF16), kn)
        row = lax.broadcasted_iota(I32, (nr, LANES), 0)
        accumulate(logits, (lane <= (row & (n_new - 1))) & (lane < n_new), v_nn=vn)
        out = acc[...] / l_s[...]
        r2 = lax.broadcasted_iota(I32, (nr, C_HEADS), 0)
        c2 = lax.broadcasted_iota(I32, (nr, C_HEADS), 1)
        own = jnp.where((r2 >> _log2(n_new)) == (c2 >> _log2(HEAD_DIM)), out, 0.0)
        o_ref[...] = jnp.sum(own.reshape(N_HEADS, n_new, C_HEADS), axis=0)


def _sample_attention(qkv, row0, n_seq, n_new, cache_kt, cache_vt, page_table, page_base):
    n_pages = page_table.shape[1]
    nr = N_HEADS * n_new
    q = qkv[row0:row0 + n_seq * n_new, :C_HEADS].reshape(n_seq, 1, n_new, C_HEADS) * (HEAD_DIM ** -0.5)
    head_of_col = (jnp.arange(C_HEADS, dtype=I32) // HEAD_DIM)[None, None, None, :]
    head_of_row = jnp.arange(N_HEADS, dtype=I32)[None, :, None, None]
    qrows = jnp.where(head_of_row == head_of_col, q, 0.0).reshape(n_seq, nr, C_HEADS)
    pt = (page_table + page_base).astype(I32)

    logits, scores = pl.pallas_call(
        _sattn_k_kernel,
        out_shape=(jax.ShapeDtypeStruct((n_seq, n_pages, nr, PAGE_SIZE), F32),
                   jax.ShapeDtypeStruct((n_seq, nr, LANES), F32)),
        grid_spec=pltpu.PrefetchScalarGridSpec(
            num_scalar_prefetch=1, grid=(n_seq, n_pages),
            in_specs=[pl.BlockSpec((1, nr, C_HEADS), lambda b, j, pt: (b, 0, 0)),
                      pl.BlockSpec((1, C_HEADS, PAGE_SIZE), lambda b, j, pt: (pt[b, j], 0, 0))],
            out_specs=(pl.BlockSpec((1, 1, nr, PAGE_SIZE), lambda b, j, pt: (b, j, 0, 0)),
                       pl.BlockSpec((1, nr, LANES), lambda b, j, pt: (b, 0, 0))),
            scratch_shapes=[pltpu.VMEM((C_HEADS, PAGE_SIZE), F32)]),
        compiler_params=_cparams(2), name="moba_sample_k",
    )(pt, qrows, cache_kt)

    rb = row0 // n_new
    return pl.pallas_call(
        functools.partial(_sattn_v_kernel, n_new=n_new),
        out_shape=jax.ShapeDtypeStruct((n_seq * n_new, C_HEADS), F32),
        grid_spec=pltpu.PrefetchScalarGridSpec(
            num_scalar_prefetch=1, grid=(n_seq, n_pages),
            in_specs=[pl.BlockSpec((1, 1, nr, PAGE_SIZE), lambda b, j, pt: (b, j, 0, 0)),
                      pl.BlockSpec((1, nr, LANES), lambda b, j, pt: (b, 0, 0)),
                      pl.BlockSpec((1, C_HEADS, PAGE_SIZE), lambda b, j, pt: (pt[b, j], 0, 0)),
                      pl.BlockSpec((1, nr, C_HEADS), lambda b, j, pt: (b, 0, 0)),
                      pl.BlockSpec((n_new, C_HEADS), lambda b, j, pt: (rb + b, 1)),
                      pl.BlockSpec((n_new, C_HEADS), lambda b, j, pt: (rb + b, 2))],
            out_specs=pl.BlockSpec((n_new, C_HEADS), lambda b, j, pt: (b, 0)),
            scratch_shapes=[pltpu.VMEM((nr, LANES), F32), pltpu.VMEM((nr, 1), F32),
                            pltpu.VMEM((nr, 1), F32), pltpu.VMEM((nr, C_HEADS), F32)]),
        compiler_params=_cparams(2), name="moba_sample_v",
    )(pt, logits, scores, cache_vt, qrows, qkv, qkv)


def _rwkv_prep_kernel(*refs, has_vres):
    if has_vres:
        (pr_ref, prev_ref, mu_ref, w0_ref, a0_ref, wup_ref, aup_ref, gup_ref, kk_ref_, ka_ref, e_ref,
         vf_ref, v0_ref, vdn_ref, vup_ref,
         r_o, w_o, k_o, v_o, kk_o, b_o, g_o) = refs
    else:
        (pr_ref, prev_ref, mu_ref, w0_ref, a0_ref, wup_ref, aup_ref, gup_ref, kk_ref_, ka_ref, e_ref,
         r_o, w_o, k_o, v_o, kk_o, b_o, g_o) = refs
    pr = pr_ref[...]
    xm = pr + (prev_ref[...] - pr) * mu_ref[...]
    c = C_HEADS
    r, k, v, lora = xm[:, :c], xm[:, c:2 * c], xm[:, 2 * c:3 * c], xm[:, 3 * c:]
    z = w0_ref[...] + _dot3(jnp.tanh(lora), wup_ref[...])
    w_log = -(jnp.maximum(-z, 0.0) + jnp.log(1.0 + jnp.exp(-jnp.abs(z)))) - 0.5
    decay = jnp.exp(-jnp.exp(w_log))
    a = _sigmoid(a0_ref[...] + _dot3(lora, aup_ref[...]))
    g = _dot3(_sigmoid(lora), gup_ref[...])
    if has_vres:
        gate = _sigmoid(v0_ref[...] + _dot3(_dot3(v, vdn_ref[...]), vup_ref[...]))
        v = v + (vf_ref[...] - v) * gate
    kk = k * kk_ref_[...]
    ss = _head_segsum(kk * kk, e_ref)
    kk = kk * lax.rsqrt(jnp.maximum(ss, 1e-24))
    r_o[...] = r
    w_o[...] = decay
    k_o[...] = k * (1.0 + (a - 1.0) * ka_ref[...])
    v_o[...] = v
    kk_o[...] = kk
    b_o[...] = kk * a
    g_o[...] = g


def _rwkv_prep(pr, prev, vecs, mats, e128, vres):
    t = pr.shape[0]
    tm = 128
    c = C_HEADS
    row = lambda w: pl.BlockSpec((tm, w), lambda i: (i, 0))
    vec = lambda w: pl.BlockSpec((1, w), lambda i: (0, 0))
    full = lambda a: pl.BlockSpec(a.shape, lambda i: (0, 0))
    mu, w0, a0, k_k, k_a = vecs
    wup, aup, gup = mats
    args = [pr, prev, mu, w0, a0, wup, aup, gup, k_k, k_a, e128]
    specs = [row(C_SHIFT_PAD), row(C_SHIFT_PAD), vec(C_SHIFT_PAD), vec(c), vec(c),
             full(wup), full(aup), full(gup), vec(c), vec(c), full(e128)]
    if vres is not None:
        v_first, v0, vdn, vup = vres
        args += [v_first, v0, vdn, vup]
        specs += [row(c), vec(c), full(vdn), full(vup)]
    return pl.pallas_call(
        functools.partial(_rwkv_prep_kernel, has_vres=vres is not None),
        out_shape=tuple(jax.ShapeDtypeStruct((t, c), F32) for _ in range(7)),
        grid=(t // tm,),
        in_specs=specs,
        out_specs=tuple(row(c) for _ in range(7)),
        compiler_params=_cparams(1), name="rwkv_prep",
    )(*args)


def _rwkv_rec_kernel(r_ref, w_ref, k_ref, v_ref, kk_ref, b_ref, s0_ref, y_ref, st_ref, state):
    nb, tc = r_ref.shape[:2]
    c = pl.program_id(1)

    @pl.when(c == 0)
    def _():
        state[...] = s0_ref[...]

    lane = lax.broadcasted_iota(I32, (HEAD_DIM, LANES), 1)
    row = lax.broadcasted_iota(I32, (HEAD_DIM, LANES), 0)
    low = lane < HEAD_DIM
    eye = jnp.where(low, lane, lane - HEAD_DIM) == row

    def seg(x):
        e = jnp.sum(jnp.where(low, x, 0.0), axis=1, keepdims=True)
        o = jnp.sum(jnp.where(low, 0.0, x), axis=1, keepdims=True)
        return jnp.where(low, e, o)

    def step(t, carry):
        for bi in range(nb):
            r_t, w_t, k_t, v_t, kk_t, b_t = (ref[bi, t] for ref in
                                             (r_ref, w_ref, k_ref, v_ref, kk_ref, b_ref))
            y_rows = []
            for p in range(N_PAIRS):
                s_old = state[bi, p]
                sa = seg(s_old * (-kk_t[p:p + 1, :]))
                v_col = seg(jnp.where(eye, v_t[p:p + 1, :], 0.0))
                s_new = s_old * w_t[p:p + 1, :] + sa * b_t[p:p + 1, :] + v_col * k_t[p:p + 1, :]
                state[bi, p] = s_new
                y_b = seg(s_new * r_t[p:p + 1, :])
                y_rows.append(jnp.sum(jnp.where(eye, y_b, 0.0), axis=0, keepdims=True))
            y_ref[bi, t] = jnp.concatenate(y_rows, axis=0)
        return carry

    lax.fori_loop(0, tc, step, 0)

    @pl.when(c == pl.num_programs(1) - 1)
    def _():
        st_ref[...] = state[...]


def _rwkv_rec(seqs, s0, nb, tc):
    b, t = seqs[0].shape[:2]
    blk = pl.BlockSpec((nb, tc, N_PAIRS, LANES), lambda g, i: (g, i, 0, 0))
    st = pl.BlockSpec((nb, N_PAIRS, HEAD_DIM, LANES), lambda g, i: (g, 0, 0, 0))
    return pl.pallas_call(
        _rwkv_rec_kernel,
        out_shape=(jax.ShapeDtypeStruct((b, t, N_PAIRS, LANES), F32),
                   jax.ShapeDtypeStruct((b, N_PAIRS, HEAD_DIM, LANES), F32)),
        grid=(b // nb, t // tc),
        in_specs=[blk] * 6 + [st],
        out_specs=(blk, st),
        scratch_shapes=[pltpu.VMEM((nb, N_PAIRS, HEAD_DIM, LANES), F32)],
        compiler_params=_cparams(2), name="rwkv_rec",
    )(*seqs, s0)


def _rwkv_post_kernel(y_ref, r_ref, k_ref, v_ref, g_ref, lnw_ref, lnb_ref, rk_ref, e_ref, o_ref):
    y = y_ref[...]
    inv = 1.0 / HEAD_DIM
    mean = _head_segsum(y, e_ref) * inv
    yc = y - mean
    var = _head_segsum(yc * yc, e_ref) * inv
    yn = yc * lax.rsqrt(var + LN_X_EPS) * lnw_ref[...] + lnb_ref[...]
    bonus = _head_segsum(r_ref[...] * k_ref[...] * rk_ref[...], e_ref) * v_ref[...]
    o_ref[...] = ((yn + bonus) * g_ref[...]).astype(o_ref.dtype)


def _rwkv_post(y, r, k, v, g, ln_w, ln_b, r_k, e128):
    t, c = y.shape
    row = pl.BlockSpec((ROW_TILE, c), lambda i: (i, 0))
    vec = pl.BlockSpec((1, c), lambda i: (0, 0))
    return pl.pallas_call(
        _rwkv_post_kernel,
        out_shape=jax.ShapeDtypeStruct((t, c), BF16),
        grid=(t // ROW_TILE,),
        in_specs=[row] * 5 + [vec] * 3 + [pl.BlockSpec(e128.shape, lambda i: (0, 0))],
        out_specs=row,
        compiler_params=_cparams(1), name="rwkv_post",
    )(y, r, k, v, g, ln_w, ln_b, r_k, e128)


def _pack_state(s):
    b = s.shape[0]
    s = s.reshape(b, N_PAIRS, 2, HEAD_DIM, HEAD_DIM)
    return jnp.transpose(s, (0, 1, 3, 2, 4)).reshape(b, N_PAIRS, HEAD_DIM, LANES)


def _unpack_state(s):
    b = s.shape[0]
    s = s.reshape(b, N_PAIRS, HEAD_DIM, 2, HEAD_DIM)
    return jnp.transpose(s, (0, 1, 3, 2, 4)).reshape(b, N_HEADS, HEAD_DIM, HEAD_DIM)


def _gather_rows_kernel(idx_ref, src_ref, o_ref, sem):
    rows = o_ref.shape[0]
    base = pl.program_id(0) * rows

    def copy(r):
        return pltpu.make_async_copy(src_ref.at[pl.ds(idx_ref[base + r], 1), :],
                                     o_ref.at[pl.ds(r, 1), :], sem)

    def start(r, carry):
        copy(r).start()
        return carry

    def wait(r, carry):
        copy(r).wait()
        return carry

    lax.fori_loop(0, rows, start, 0)
    lax.fori_loop(0, rows, wait, 0)


def _gather_rows(src, idx, rows_per_step):
    n = idx.shape[0]
    w = src.shape[1]
    return pl.pallas_call(
        _gather_rows_kernel,
        out_shape=jax.ShapeDtypeStruct((n, w), src.dtype),
        grid_spec=pltpu.PrefetchScalarGridSpec(
            num_scalar_prefetch=1, grid=(n // rows_per_step,),
            in_specs=[pl.BlockSpec(memory_space=pl.ANY)],
            out_specs=pl.BlockSpec((rows_per_step, w), lambda i, idx: (i, 0)),
            scratch_shapes=[pltpu.SemaphoreType.DMA]),
        compiler_params=_cparams(1), name="gather_rows",
    )(idx, src)


def _expert_kernel(te_ref, nu_ref, x_ref, w1_ref, b1_ref, w2_ref, b2_ref, sel_ref, o_ref, xb):
    t = pl.program_id(0)
    j = pl.program_id(1)
    nf = pl.num_programs(1)

    @pl.when(t < nu_ref[0])
    def _():
        @pl.when(j == 0)
        def _():
            xb[...] = x_ref[...].astype(BF16)

        hh = _dot(xb[...], w1_ref[0].astype(BF16)) + b1_ref[0]
        width = hh.shape[1]
        nxt = pltpu.roll(hh, width - 1, 1)
        glu = jnp.minimum(hh, SWIGLU_LIMIT)
        lin = jnp.clip(nxt, -SWIGLU_LIMIT, SWIGLU_LIMIT)
        act = glu * _sigmoid(SWIGLU_ALPHA * glu) * (lin + 1.0)
        act = _dot(act.astype(BF16), sel_ref[...]).astype(BF16)
        part = _dot(act, w2_ref[0].astype(BF16))

        @pl.when(j == 0)
        def _():
            o_ref[...] = part + b2_ref[0]

        @pl.when(j > 0)
        def _():
            o_ref[...] = o_ref[...] + part

    @pl.when((t >= nu_ref[0]) & (j == 0))
    def _():
        o_ref[...] = jnp.zeros(o_ref.shape, F32)


def _expert_mlp(xs, tile_expert, n_used, w1, b1, w2, b2):
    n_rows = xs.shape[0]
    tm, bf = EXPERT_TILE, EXPERT_FBLOCK
    n_tiles = n_rows // tm
    nf = D_EXPERT // bf
    sel = (jnp.arange(2 * bf, dtype=I32)[:, None] == 2 * jnp.arange(bf, dtype=I32)[None, :]).astype(BF16)
    last = lambda t, nu: jnp.minimum(t, nu[0] - 1)
    return pl.pallas_call(
        _expert_kernel,
        out_shape=jax.ShapeDtypeStruct((n_rows, D_MODEL), F32),
        grid_spec=pltpu.PrefetchScalarGridSpec(
            num_scalar_prefetch=2, grid=(n_tiles, nf),
            in_specs=[pl.BlockSpec((tm, D_MODEL), lambda t, j, te, nu: (last(t, nu), 0)),
                      pl.BlockSpec((1, D_MODEL, 2 * bf), lambda t, j, te, nu: (te[t], 0, j)),
                      pl.BlockSpec((1, 1, 2 * bf), lambda t, j, te, nu: (te[t], 0, j)),
                      pl.BlockSpec((1, bf, D_MODEL), lambda t, j, te, nu: (te[t], j, 0)),
                      pl.BlockSpec((1, 1, D_MODEL), lambda t, j, te, nu: (te[t], 0, 0)),
                      pl.BlockSpec((2 * bf, bf), lambda t, j, te, nu: (0, 0))],
            out_specs=pl.BlockSpec((tm, D_MODEL), lambda t, j, te, nu: (t, 0)),
            scratch_shapes=[pltpu.VMEM((tm, D_MODEL), BF16)]),
        compiler_params=_cparams(2), name="expert_mlp",
    )(tile_expert, n_used, xs, w1, b1.reshape(N_EXPERTS, 1, 2 * D_EXPERT), w2,
      b2.reshape(N_EXPERTS, 1, D_MODEL), sel)


def _combine_kernel(slot_ref, ys_ref, x_ref, route_ref, gt_ref, o_ref, ybuf, sem, *, grp):
    rows = x_ref.shape[0]
    base = pl.program_id(0) * rows

    def copy(r, j):
        return pltpu.make_async_copy(ys_ref.at[pl.ds(slot_ref[(base + r) * TOP_K + j], 1), :],
                                     ybuf.at[j, pl.ds(r, 1), :], sem)

    def start(r, carry):
        for j in range(TOP_K):
            copy(r, j).start()
        return carry

    def wait(r, carry):
        for j in range(TOP_K):
            copy(r, j).wait()
        return carry

    lax.fori_loop(0, rows, start, 0)
    lax.fori_loop(0, rows, wait, 0)
    route = route_ref[...]
    moe = jnp.zeros(x_ref.shape, F32)
    for j in range(TOP_K):
        moe = moe + route[:, TOP_K + j:TOP_K + j + 1] * ybuf[j]
    oh = _seq_onehot(base, rows, grp)
    o_ref[...] = x_ref[...] + _dot_exact_lhs(oh, gt_ref[...]) * moe


def _combine(x, ys, slots, route, mod, k_gt, grp):
    t = x.shape[0]
    tm = 128
    return pl.pallas_call(
        functools.partial(_combine_kernel, grp=grp),
        out_shape=jax.ShapeDtypeStruct((t, D_MODEL), F32),
        grid_spec=pltpu.PrefetchScalarGridSpec(
            num_scalar_prefetch=1, grid=(t // tm,),
            in_specs=[pl.BlockSpec(memory_space=pl.ANY),
                      pl.BlockSpec((tm, D_MODEL), lambda i, s: (i, 0)),
                      pl.BlockSpec((tm, LANES), lambda i, s: (i, 0)),
                      pl.BlockSpec((SEQ_TABLE_ROWS, D_MODEL), lambda i, s, k=k_gt: (0, k))],
            out_specs=pl.BlockSpec((tm, D_MODEL), lambda i, s: (i, 0)),
            scratch_shapes=[pltpu.VMEM((TOP_K, tm, D_MODEL), F32), pltpu.SemaphoreType.DMA]),
        compiler_params=_cparams(1), name="moe_combine",
    )(slots, ys, x, route, mod)


def _moe_layer(x, g, mod, wr_pad, br_pad, w1, b1, w2, b2, grp):
    t = x.shape[0]
    h2, route, counts = _rms_router(x, g, mod, 4, 3, wr_pad, br_pad, grp)
    tm = EXPERT_TILE
    n_tiles = (t * TOP_K) // tm + N_EXPERTS
    cnt = counts[0, :N_EXPERTS].astype(I32)
    tiles_per = (cnt + tm - 1) // tm
    tile_end = jnp.cumsum(tiles_per)
    group_start = (tile_end - tiles_per) * tm
    idx = route[:, :TOP_K].astype(I32)
    pos = route[:, 2 * TOP_K:3 * TOP_K].astype(I32)
    slots = (group_start[idx] + pos).reshape(-1)
    token_of_pair = jnp.repeat(jnp.arange(t, dtype=I32), TOP_K)
    src = jnp.zeros((n_tiles * tm,), I32).at[slots].set(token_of_pair)
    n_used = tile_end[-1:].astype(I32)
    tile_expert = jnp.searchsorted(tile_end, jnp.arange(n_tiles, dtype=I32), side="right").astype(I32)
    tile_expert = jnp.minimum(tile_expert, N_EXPERTS - 1)
    tile_expert = jnp.where(jnp.arange(n_tiles) < n_used[0], tile_expert,
                            tile_expert[jnp.maximum(n_used[0] - 1, 0)])
    xs = _gather_rows(h2, src, 256)
    ys = _expert_mlp(xs, tile_expert, n_used, w1, b1, w2, b2)
    return _combine(x, ys, slots, route, mod, 5, grp)


def _log2(n):
    s = n.bit_length() - 1
    assert (1 << s) == n, "sequence lengths must be powers of two"
    return s


def kernel(x_prompt, x_sample, cache_k, cache_v, page_table, state_shift, state_wkv, c_prompt, c_sample,
           w_ada, b_ada, norm_mix, norm_ffn, w_in, w_out, rw_mu, rw_w0, rw_w_up, rw_a0, rw_a_up,
           rw_g_up, rw_k_k, rw_k_a, rw_r_k, rw_ln_w, rw_ln_b, rw_v0, rw_v_down, rw_v_up,
           w_router, b_router, w_mlp1, b_mlp1, w_mlp2, b_mlp2, norm_final):
    bp, tp, d = x_prompt.shape
    bs, ts, _ = x_sample.shape
    depth = w_ada.shape[0]
    n_pool = cache_k.shape[1]
    np_rows, ns_rows = bp * tp, bs * ts
    t_all = np_rows + ns_rows
    assert d == D_MODEL and t_all % ROW_TILE == 0 and np_rows % ROW_TILE == 0
    assert bp + bs <= SEQ_TABLE_ROWS and tp % MOBA_BLOCK == 0 and ts <= SUBLANES
    grp = (np_rows, _log2(tp), _log2(ts), bp)
    c = C_HEADS

    x = jnp.concatenate([x_prompt.reshape(np_rows, d), x_sample.reshape(ns_rows, d)], axis=0)
    c_pad = jnp.zeros((SEQ_TABLE_ROWS, d), F32).at[:bp + bs].set(jnp.concatenate([c_prompt, c_sample], axis=0))
    mod = _ada(c_pad, w_ada, b_ada)

    cache_kt = jnp.transpose(cache_k, (0, 1, 3, 4, 2)).reshape(depth * n_pool, c, PAGE_SIZE)
    cache_vt = jnp.transpose(cache_v, (0, 1, 3, 4, 2)).reshape(depth * n_pool, c, PAGE_SIZE)
    e128 = (jnp.arange(LANES)[:, None] // HEAD_DIM == jnp.arange(LANES)[None, :] // HEAD_DIM).astype(BF16)

    def lora_pad(w, start):
        return jnp.zeros((C_LORA_PAD, c), F32).at[start:start + w.shape[0]].set(w)

    k_out, v_out, shift_p, shift_s, wkv_p, wkv_s = [], [], [], [], [], []
    v_first = None
    for l in range(depth):
        h = _rms_mod(x, norm_mix[l], mod[l], 1, 0, grp)
        w_qkv = w_in[l][:, :3 * c].astype(BF16)
        w_pr = jnp.zeros((d, C_SHIFT_PAD), BF16).at[:, :C_SHIFT].set(w_in[l][:, 3 * c:].astype(BF16))
        qkv = _matmul(h, w_qkv, 768, 1024)
        pr = _matmul(h, w_pr, 768, 512)

        att_p = _prompt_attention(qkv, bp, tp)
        att_s = _sample_attention(qkv, np_rows, bs, ts, cache_kt, cache_vt, page_table, l * n_pool)
        att = jnp.concatenate([att_p, att_s.astype(BF16)], axis=0)

        pr_p = pr[:np_rows].reshape(bp, tp, C_SHIFT_PAD)
        pr_s = pr[np_rows:].reshape(bs, ts, C_SHIFT_PAD)
        shift0_s = jnp.zeros((bs, 1, C_SHIFT_PAD), F32).at[:, 0, :C_SHIFT].set(state_shift[l])
        prev = jnp.concatenate([
            jnp.concatenate([jnp.zeros((bp, 1, C_SHIFT_PAD), F32), pr_p[:, :-1]], axis=1).reshape(np_rows, -1),
            jnp.concatenate([shift0_s, pr_s[:, :-1]], axis=1).reshape(ns_rows, -1)], axis=0)
        mu = jnp.zeros((1, C_SHIFT_PAD), F32).at[0, :C_SHIFT].set(rw_mu[l])
        vecs = (mu, rw_w0[l].reshape(1, c), rw_a0[l].reshape(1, c), rw_k_k[l].reshape(1, c),
                rw_k_a[l].reshape(1, c))
        mats = (lora_pad(rw_w_up[l], 0), lora_pad(rw_a_up[l], D_DECAY_LORA),
                lora_pad(rw_g_up[l], D_DECAY_LORA + D_AAA_LORA))
        vres = None
        if l > 0:
            vdn = jnp.zeros((c, LANES), F32).at[:, :D_MV_LORA].set(rw_v_down[l - 1])
            vup = jnp.zeros((LANES, c), F32).at[:D_MV_LORA].set(rw_v_up[l - 1])
            vres = (v_first, rw_v0[l - 1].reshape(1, c), vdn, vup)
        r, w, k, v, kk, b, g = _rwkv_prep(pr, prev, vecs, mats, e128, vres)
        if l == 0:
            v_first = v
        seqs_p = [a[:np_rows].reshape(bp, tp, N_PAIRS, LANES) for a in (r, w, k, v, kk, b)]
        seqs_s = [a[np_rows:].reshape(bs, ts, N_PAIRS, LANES) for a in (r, w, k, v, kk, b)]
        y_p, st_p = _rwkv_rec(seqs_p, jnp.zeros((bp, N_PAIRS, HEAD_DIM, LANES), F32), bp, 64)
        y_s, st_s = _rwkv_rec(seqs_s, _pack_state(state_wkv[l]), 4, ts)
        y = jnp.concatenate([y_p.reshape(np_rows, c), y_s.reshape(ns_rows, c)], axis=0)
        rw = _rwkv_post(y, r, k, v, g, rw_ln_w[l].reshape(1, c), rw_ln_b[l].reshape(1, c),
                        rw_r_k[l].reshape(1, c), e128)

        x = _mix_out(att, rw, w_out[l].astype(BF16), x, mod[l], 2, grp)

        wr_pad = jnp.zeros((d, LANES), F32).at[:, :N_EXPERTS].set(w_router[l])
        br_pad = jnp.full((1, LANES), NEG_BIG, F32).at[0, :N_EXPERTS].set(b_router[l])
        x = _moe_layer(x, norm_ffn[l], mod[l], wr_pad, br_pad, w_mlp1[l], b_mlp1[l], w_mlp2[l],
                       b_mlp2[l], grp)

        k_out.append(qkv[:, c:2 * c])
        v_out.append(qkv[:, 2 * c:3 * c])
        shift_p.append(pr_p[:, -1, :C_SHIFT])
        shift_s.append(pr_s[:, -1, :C_SHIFT])
        wkv_p.append(_unpack_state(st_p))
        wkv_s.append(_unpack_state(st_s))

    y = _final_norm(x, norm_final)
    k_all, v_all = jnp.stack(k_out), jnp.stack(v_out)
    pages = tp // PAGE_SIZE
    return (y[:np_rows].reshape(bp, tp, d), y[np_rows:].reshape(bs, ts, d),
            k_all[:, :np_rows].reshape(depth, bp, pages, PAGE_SIZE, N_HEADS, HEAD_DIM),
            v_all[:, :np_rows].reshape(depth, bp, pages, PAGE_SIZE, N_HEADS, HEAD_DIM),
            jnp.stack(shift_p), jnp.stack(wkv_p),
            k_all[:, np_rows:].reshape(depth, bs, ts, N_HEADS, HEAD_DIM),
            v_all[:, np_rows:].reshape(depth, bs, ts, N_HEADS, HEAD_DIM),
            jnp.stack(shift_s), jnp.stack(wkv_s))
```

```python
import functools

import jax
import jax.numpy as jnp
from jax import lax
from jax.experimental import pallas as pl
from jax.experimental.pallas import tpu as pltpu

F32, BF16, I32 = jnp.float32, jnp.bfloat16, jnp.int32

LANES = 128
SUBLANES = 8
VMEM_LIMIT = 50 * 1024 * 1024

D_MODEL = 2048
HEAD_DIM = 64
N_HEADS = 16
C_HEADS = N_HEADS * HEAD_DIM
N_PAIRS = C_HEADS // LANES
MOBA_BLOCK = 256
MOBA_TOPK = 3
PAGE_SIZE = 128
D_DECAY_LORA, D_AAA_LORA, D_GATE_LORA, D_MV_LORA = 64, 64, 160, 32
C_SHIFT = 3 * C_HEADS + D_DECAY_LORA + D_AAA_LORA + D_GATE_LORA
C_SHIFT_PAD = 3584
C_LORA_PAD = C_SHIFT_PAD - 3 * C_HEADS
LN_X_EPS = 64e-5
RMS_EPS = 1e-5
N_EXPERTS = 32
TOP_K = 4
D_EXPERT = D_MODEL
SWIGLU_LIMIT = 7.0
SWIGLU_ALPHA = 1.702
NEG_BIG = -3.0e38

ROW_TILE = 256
SEQ_TABLE_ROWS = 128
EXPERT_TILE = 512
EXPERT_FBLOCK = 256
RWKV_CHUNK = 64
RWKV_PAIRS_PER_STEP = 8
SAMPLE_PAGES_PER_STEP = 8


def _cparams(n_axes):
    return pltpu.CompilerParams(dimension_semantics=("arbitrary",) * n_axes,
                                vmem_limit_bytes=VMEM_LIMIT)


def _dot(a, b):
    return jnp.dot(a, b, preferred_element_type=F32)


def _dot_nt(a, b):
    return lax.dot_general(a, b, (((1,), (1,)), ((), ())), preferred_element_type=F32)


def _split2(x):
    hi = x.astype(BF16)
    lo = (x - hi.astype(F32)).astype(BF16)
    return hi, lo


def _split3(x):
    hi = x.astype(BF16)
    r = x - hi.astype(F32)
    mid = r.astype(BF16)
    lo = (r - mid.astype(F32)).astype(BF16)
    return hi, mid, lo


def _dot_ref(a, b, nt=False):
    d = _dot_nt if nt else _dot
    return d(a.astype(BF16), b.astype(BF16))


def _dot_exact_lhs(a_bf16, b):
    bh, bm, bl = _split3(b)
    return _dot(a_bf16, bh) + (_dot(a_bf16, bm) + _dot(a_bf16, bl))


def _seq_onehot(row0, rows, grp):
    n_prompt_rows, tp_shift, ts_shift, n_prompt = grp
    r = row0 + lax.broadcasted_iota(I32, (rows, SEQ_TABLE_ROWS), 0)
    lane = lax.broadcasted_iota(I32, (rows, SEQ_TABLE_ROWS), 1)
    sid = jnp.where(r < n_prompt_rows, r >> tp_shift, n_prompt + ((r - n_prompt_rows) >> ts_shift))
    return jnp.where(lane == sid, 1.0, 0.0).astype(BF16)


def _head_segsum(x, e_ref):
    e = e_ref[...]
    outs = []
    for c in range(x.shape[1] // LANES):
        hi, lo = _split2(x[:, c * LANES:(c + 1) * LANES])
        outs.append(_dot(hi, e) + _dot(lo, e))
    return jnp.concatenate(outs, axis=1)


def _sigmoid(x):
    return 1.0 / (1.0 + jnp.exp(-x))


def _ada_kernel(c_ref, w_ref, b_ref, o_ref):
    o_ref[0] = _dot_ref(c_ref[...], w_ref[0]) + b_ref[0]


def _ada(c_pad, w_ada, b_ada):
    depth, d, n = w_ada.shape
    tn = 512
    return pl.pallas_call(
        _ada_kernel,
        out_shape=jax.ShapeDtypeStruct((depth, SEQ_TABLE_ROWS, n), F32),
        grid=(depth, n // tn),
        in_specs=[pl.BlockSpec((SEQ_TABLE_ROWS, d), lambda l, j: (0, 0)),
                  pl.BlockSpec((1, d, tn), lambda l, j: (l, 0, j)),
                  pl.BlockSpec((1, 1, tn), lambda l, j: (l, 0, j))],
        out_specs=pl.BlockSpec((1, SEQ_TABLE_ROWS, tn), lambda l, j: (l, 0, j)),
        compiler_params=_cparams(2), name="ada_mod",
    )(c_pad, w_ada, b_ada.reshape(depth, 1, n))


def _rms_modulated(x_ref, g_ref, sc_ref, sh_ref, grp):
    rows = x_ref.shape[0]
    x = x_ref[...]
    y = x * lax.rsqrt(jnp.mean(x * x, axis=-1, keepdims=True) + RMS_EPS) * g_ref[...]
    oh = _seq_onehot(pl.program_id(0) * rows, rows, grp)
    sc = _dot_exact_lhs(oh, sc_ref[...])
    sh = _dot_exact_lhs(oh, sh_ref[...])
    return y * (1.0 + sc) + sh


def _rms_mod_kernel(x_ref, g_ref, sc_ref, sh_ref, o_ref, *, grp):
    o_ref[...] = _rms_modulated(x_ref, g_ref, sc_ref, sh_ref, grp).astype(o_ref.dtype)


def _mod_spec(k):
    return pl.BlockSpec((SEQ_TABLE_ROWS, D_MODEL), lambda i, k=k: (0, k))


def _rms_mod(x, g, mod, k_sc, k_sh, grp):
    t = x.shape[0]
    return pl.pallas_call(
        functools.partial(_rms_mod_kernel, grp=grp),
        out_shape=jax.ShapeDtypeStruct((t, D_MODEL), BF16),
        grid=(t // ROW_TILE,),
        in_specs=[pl.BlockSpec((ROW_TILE, D_MODEL), lambda i: (i, 0)),
                  pl.BlockSpec((1, D_MODEL), lambda i: (0, 0)),
                  _mod_spec(k_sc), _mod_spec(k_sh)],
        out_specs=pl.BlockSpec((ROW_TILE, D_MODEL), lambda i: (i, 0)),
        compiler_params=_cparams(1), name="rms_mod",
    )(x, g.reshape(1, D_MODEL), mod, mod)


def _rms_router_kernel(x_ref, g_ref, sc_ref, sh_ref, wr_ref, br_ref, h_ref, route_ref, cnt_ref,
                       carry, *, grp):
    i = pl.program_id(0)
    rows = x_ref.shape[0]
    h = _rms_modulated(x_ref, g_ref, sc_ref, sh_ref, grp)
    h_ref[...] = h
    logits = _dot_ref(h, wr_ref[...]) + br_ref[...]
    lane = lax.broadcasted_iota(I32, (rows, LANES), 1)
    vals, idxs = [], []
    multi = jnp.zeros((rows, LANES), F32)
    for _ in range(TOP_K):
        m = jnp.max(logits, axis=1, keepdims=True)
        sel = jnp.min(jnp.where(logits == m, lane, LANES), axis=1, keepdims=True)
        hit = lane == sel
        vals.append(m)
        idxs.append(sel)
        multi = jnp.where(hit, 1.0, multi)
        logits = jnp.where(hit, NEG_BIG, logits)
    es = [jnp.exp(v - vals[0]) for v in vals]
    denom = es[0] + es[1] + es[2] + es[3]

    @pl.when(i == 0)
    def _():
        carry[...] = jnp.zeros_like(carry)

    r_i = lax.broadcasted_iota(I32, (rows, rows), 0)
    c_i = lax.broadcasted_iota(I32, (rows, rows), 1)
    tri = jnp.where(c_i < r_i, 1.0, 0.0).astype(BF16)
    cum = _dot(tri, multi.astype(BF16)) + carry[0:1, :]
    route = jnp.zeros((rows, LANES), F32)
    for j in range(TOP_K):
        pos = jnp.sum(jnp.where(lane == idxs[j], cum, 0.0), axis=1, keepdims=True)
        route = jnp.where(lane == j, idxs[j].astype(F32), route)
        route = jnp.where(lane == TOP_K + j, es[j] / denom, route)
        route = jnp.where(lane == 2 * TOP_K + j, pos, route)
    route_ref[...] = route
    new_cnt = carry[0:1, :] + jnp.sum(multi, axis=0, keepdims=True)
    carry[...] = jnp.broadcast_to(new_cnt, carry.shape)
    cnt_ref[...] = jnp.broadcast_to(new_cnt, cnt_ref.shape)


def _rms_router(x, g, mod, k_sc, k_sh, wr_pad, br_pad, grp):
    t = x.shape[0]
    return pl.pallas_call(
        functools.partial(_rms_router_kernel, grp=grp),
        out_shape=(jax.ShapeDtypeStruct((t, D_MODEL), F32),
                   jax.ShapeDtypeStruct((t, LANES), F32),
                   jax.ShapeDtypeStruct((SUBLANES, LANES), F32)),
        grid=(t // ROW_TILE,),
        in_specs=[pl.BlockSpec((ROW_TILE, D_MODEL), lambda i: (i, 0)),
                  pl.BlockSpec((1, D_MODEL), lambda i: (0, 0)),
                  _mod_spec(k_sc), _mod_spec(k_sh),
                  pl.BlockSpec((D_MODEL, LANES), lambda i: (0, 0)),
                  pl.BlockSpec((1, LANES), lambda i: (0, 0))],
        out_specs=(pl.BlockSpec((ROW_TILE, D_MODEL), lambda i: (i, 0)),
                   pl.BlockSpec((ROW_TILE, LANES), lambda i: (i, 0)),
                   pl.BlockSpec((SUBLANES, LANES), lambda i: (0, 0))),
        scratch_shapes=[pltpu.VMEM((SUBLANES, LANES), F32)],
        compiler_params=_cparams(1), name="rms_router",
    )(x, g.reshape(1, D_MODEL), mod, mod, wr_pad, br_pad)


def _final_norm_kernel(x_ref, g_ref, o_ref):
    x = x_ref[...]
    o_ref[...] = x * lax.rsqrt(jnp.mean(x * x, axis=-1, keepdims=True) + RMS_EPS) * g_ref[...]


def _final_norm(x, g):
    t = x.shape[0]
    return pl.pallas_call(
        _final_norm_kernel,
        out_shape=jax.ShapeDtypeStruct((t, D_MODEL), F32),
        grid=(t // ROW_TILE,),
        in_specs=[pl.BlockSpec((ROW_TILE, D_MODEL), lambda i: (i, 0)),
                  pl.BlockSpec((1, D_MODEL), lambda i: (0, 0))],
        out_specs=pl.BlockSpec((ROW_TILE, D_MODEL), lambda i: (i, 0)),
        compiler_params=_cparams(1), name="final_norm",
    )(x, g.reshape(1, D_MODEL))


def _mm_kernel(a_ref, w_ref, o_ref):
    o_ref[...] = _dot(a_ref[...], w_ref[...])


def _matmul(a, w, tm, tn):
    m, k = a.shape
    n = w.shape[1]
    return pl.pallas_call(
        _mm_kernel,
        out_shape=jax.ShapeDtypeStruct((m, n), F32),
        grid=(n // tn, m // tm),
        in_specs=[pl.BlockSpec((tm, k), lambda j, i: (i, 0)),
                  pl.BlockSpec((k, tn), lambda j, i: (0, j))],
        out_specs=pl.BlockSpec((tm, tn), lambda j, i: (i, j)),
        compiler_params=_cparams(2), name="proj_matmul",
    )(a, w)


def _mix_out_kernel(att_ref, rw_ref, wa_ref, wb_ref, x_ref, gt_ref, o_ref, *, grp):
    rows = x_ref.shape[0]
    mix = _dot(att_ref[...], wa_ref[...]) + _dot(rw_ref[...], wb_ref[...])
    oh = _seq_onehot(pl.program_id(1) * rows, rows, grp)
    o_ref[...] = x_ref[...] + _dot_exact_lhs(oh, gt_ref[...]) * mix


def _mix_out(att, rw, w_out, x, mod, k_gt, grp):
    t = x.shape[0]
    tm, tn = ROW_TILE, 1024
    nb = D_MODEL // tn
    return pl.pallas_call(
        functools.partial(_mix_out_kernel, grp=grp),
        out_shape=jax.ShapeDtypeStruct((t, D_MODEL), F32),
        grid=(nb, t // tm),
        in_specs=[pl.BlockSpec((tm, C_HEADS), lambda j, i: (i, 0)),
                  pl.BlockSpec((tm, C_HEADS), lambda j, i: (i, 0)),
                  pl.BlockSpec((C_HEADS, tn), lambda j, i: (0, j)),
                  pl.BlockSpec((C_HEADS, tn), lambda j, i: (1, j)),
                  pl.BlockSpec((tm, tn), lambda j, i: (i, j)),
                  pl.BlockSpec((SEQ_TABLE_ROWS, tn), lambda j, i, k=k_gt, nb=nb: (0, k * nb + j))],
        out_specs=pl.BlockSpec((tm, tn), lambda j, i: (i, j)),
        compiler_params=_cparams(2), name="mix_out",
    )(att, rw, w_out, w_out, x, mod)


def _pattn_kernel(q_ref, k_ref, v_ref, o_ref, kmean):
    blk = q_ref.shape[0]
    n_blk = k_ref.shape[0] // blk
    qi = pl.program_id(2)
    lane = lax.broadcasted_iota(I32, (blk, LANES), 1)
    low = lane < HEAD_DIM

    @pl.when(qi == 0)
    def _():
        rows = [jnp.sum(k_ref[n * blk:(n + 1) * blk, :], axis=0, keepdims=True) * (1.0 / blk)
                for n in range(n_blk)]
        rows.append(jnp.zeros((LANES - n_blk, LANES), F32))
        kmean[...] = jnp.concatenate(rows, axis=0)

    q = q_ref[...] * (HEAD_DIM ** -0.5)
    qh = (jnp.where(low, q, 0.0), jnp.where(low, 0.0, q))
    km = kmean[...]
    selm = []
    for h in range(2):
        s = _dot_ref(qh[h], km, nt=True)
        cnt = jnp.zeros((blk, LANES), F32)
        for m in range(n_blk):
            sm = s[:, m:m + 1]
            beats = (sm > s) | ((sm == s) & (m < lane))
            cnt = cnt + jnp.where(beats, 1.0, 0.0) * jnp.where(m < qi, 1.0, 0.0)
        selm.append(jnp.where((cnt < MOBA_TOPK) & (lane < qi), 1.0, 0.0))
    qb = (qh[0].astype(BF16), qh[1].astype(BF16))
    row_i = lax.broadcasted_iota(I32, (blk, blk), 0)
    col_i = lax.broadcasted_iota(I32, (blk, blk), 1)
    causal = jnp.where(col_i <= row_i, 1.0, 0.0)

    def body(n, carry):
        off = pl.multiple_of(n * blk, blk)
        kb = k_ref[pl.ds(off, blk), :].astype(BF16)
        vb = v_ref[pl.ds(off, blk), :].astype(BF16)
        out = []
        for h in range(2):
            m_old, l_old, acc = carry[3 * h:3 * h + 3]
            logits = _dot_nt(qb[h], kb)
            flag = jnp.sum(jnp.where(lane == n, selm[h], 0.0), axis=1, keepdims=True)
            own = jnp.where(n == qi, 1.0, 0.0)
            allowed = (own * causal + (1.0 - own) * flag) > 0.5
            masked = jnp.where(allowed, logits, NEG_BIG)
            m_new = jnp.maximum(m_old, jnp.max(masked, axis=1, keepdims=True))
            alpha = jnp.exp(m_old - m_new)
            p = jnp.where(allowed, jnp.exp(masked - m_new), 0.0)
            l_new = l_old * alpha + jnp.sum(p, axis=1, keepdims=True)
            acc = acc * alpha + _dot(p.astype(BF16), vb)
            out += [m_new, l_new, acc]
        return tuple(out)

    init = (jnp.full((blk, 1), NEG_BIG, F32), jnp.zeros((blk, 1), F32), jnp.zeros((blk, LANES), F32)) * 2
    res = lax.fori_loop(0, qi + 1, body, init)
    o_ref[...] = jnp.where(low, res[2] / res[1], res[5] / res[4]).astype(o_ref.dtype)


def _prompt_attention(qkv, n_seq, seq_len):
    blk = MOBA_BLOCK
    nq = seq_len // blk
    return pl.pallas_call(
        _pattn_kernel,
        out_shape=jax.ShapeDtypeStruct((n_seq * seq_len, C_HEADS), BF16),
        grid=(n_seq, N_PAIRS, nq),
        in_specs=[pl.BlockSpec((blk, LANES), lambda b, p, i: (b * nq + i, p)),
                  pl.BlockSpec((seq_len, LANES), lambda b, p, i: (b, N_PAIRS + p)),
                  pl.BlockSpec((seq_len, LANES), lambda b, p, i: (b, 2 * N_PAIRS + p))],
        out_specs=pl.BlockSpec((blk, LANES), lambda b, p, i: (b * nq + i, p)),
        scratch_shapes=[pltpu.VMEM((LANES, LANES), F32)],
        compiler_params=_cparams(3), name="moba_prompt",
    )(qkv, qkv, qkv)


def _sattn_k_kernel(pt_ref, qr_ref, *refs, n_grp):
    kp_refs, (lg_ref, sc_ref) = refs[:n_grp], refs[n_grp:]
    j = pl.program_id(1)
    qr = qr_ref[0]
    qb = qr.astype(BF16)
    pages_per_block = MOBA_BLOCK // PAGE_SIZE
    blocks_per_step = n_grp // pages_per_block
    lane = lax.broadcasted_iota(I32, sc_ref.shape[1:], 1)

    @pl.when(j == 0)
    def _():
        sc_ref[0] = jnp.zeros(sc_ref.shape[1:], F32)

    kts = [ref[0] for ref in kp_refs]
    for u in range(n_grp):
        lg_ref[0, u] = _dot(qb, kts[u].astype(BF16))
    sc = sc_ref[0]
    for n in range(blocks_per_step):
        ksum = kts[n * pages_per_block]
        for u in range(1, pages_per_block):
            ksum = ksum + kts[n * pages_per_block + u]
        kmean = jnp.sum(ksum, axis=1, keepdims=True) * (1.0 / MOBA_BLOCK)
        s = _dot(qb, jnp.broadcast_to(kmean, (C_HEADS, LANES)).astype(BF16))
        sc = jnp.where(lane == j * blocks_per_step + n, s, sc)
    sc_ref[0] = sc


def _sattn_v_kernel(pt_ref, lg_ref, sc_ref, *refs, n_new, n_grp, n_blocks):
    vp_refs = refs[:n_grp]
    qr_ref, kn_ref, vn_ref, o_ref, selm, m_s, l_s, acc = refs[n_grp:]
    j = pl.program_id(1)
    nr = lg_ref.shape[2]
    lane = lax.broadcasted_iota(I32, (nr, LANES), 1)
    pages_per_block = MOBA_BLOCK // PAGE_SIZE

    @pl.when(j == 0)
    def _():
        s = sc_ref[0]
        cnt = jnp.zeros((nr, LANES), F32)
        for m in range(n_blocks):
            sm = s[:, m:m + 1]
            cnt = cnt + jnp.where((sm > s) | ((sm == s) & (m < lane)), 1.0, 0.0)
        selm[...] = jnp.where((cnt < MOBA_TOPK) & (lane < n_blocks), 1.0, 0.0)
        m_s[...] = jnp.full(m_s.shape, NEG_BIG, F32)
        l_s[...] = jnp.zeros(l_s.shape, F32)
        acc[...] = jnp.zeros(acc.shape, F32)

    def accumulate(logits, allowed, pv_fns):
        masked = [jnp.where(a, x, NEG_BIG) for x, a in zip(logits, allowed)]
        m_old = m_s[...]
        m_new = m_old
        for x in masked:
            m_new = jnp.maximum(m_new, jnp.max(x, axis=1, keepdims=True))
        alpha = jnp.exp(m_old - m_new)
        l_new = l_s[...] * alpha
        pv = acc[...] * alpha
        for x, a, fn in zip(masked, allowed, pv_fns):
            p = jnp.where(a, jnp.exp(x - m_new), 0.0)
            l_new = l_new + jnp.sum(p, axis=1, keepdims=True)
            pv = pv + fn(p.astype(BF16))
        l_s[...] = l_new
        acc[...] = pv
        m_s[...] = m_new

    sel = selm[...]
    flags = [jnp.sum(jnp.where(lane == (j * n_grp + u) // pages_per_block, sel, 0.0), axis=1, keepdims=True)
             for u in range(n_grp)]
    accumulate([lg_ref[0, u] for u in range(n_grp)],
               [jnp.broadcast_to(f, (nr, LANES)) > 0.0 for f in flags],
               [functools.partial(lambda p, ref: _dot_nt(p, ref[0].astype(BF16)), ref=ref) for ref in vp_refs])

    @pl.when(j == pl.num_programs(1) - 1)
    def _():
        pad = jnp.zeros((LANES - n_new, C_HEADS), F32)
        kn = jnp.concatenate([kn_ref[...], pad], axis=0).astype(BF16)
        vn = jnp.concatenate([vn_ref[...], pad], axis=0).astype(BF16)
        logits = _dot_nt(qr_ref[0].astype(BF16), kn)
        row = lax.broadcasted_iota(I32, (nr, LANES), 0)
        accumulate([logits], [(lane <= (row & (n_new - 1))) & (lane < n_new)], [lambda p: _dot(p, vn)])
        out = acc[...] / l_s[...]
        r2 = lax.broadcasted_iota(I32, (nr, C_HEADS), 0)
        c2 = lax.broadcasted_iota(I32, (nr, C_HEADS), 1)
        own = jnp.where((r2 >> _log2(n_new)) == (c2 >> _log2(HEAD_DIM)), out, 0.0)
        o_ref[...] = jnp.sum(own.reshape(N_HEADS, n_new, C_HEADS), axis=0)


def _sample_attention(qkv, row0, n_seq, n_new, cache_kt, cache_vt, page_table, page_base):
    n_pages = page_table.shape[1]
    n_grp = SAMPLE_PAGES_PER_STEP
    n_blocks = (n_pages * PAGE_SIZE) // MOBA_BLOCK
    assert n_pages % n_grp == 0 and n_grp % (MOBA_BLOCK // PAGE_SIZE) == 0 and n_blocks <= LANES
    n_steps = n_pages // n_grp
    nr = N_HEADS * n_new
    q = qkv[row0:row0 + n_seq * n_new, :C_HEADS].reshape(n_seq, 1, n_new, C_HEADS) * (HEAD_DIM ** -0.5)
    head_of_col = (jnp.arange(C_HEADS, dtype=I32) // HEAD_DIM)[None, None, None, :]
    head_of_row = jnp.arange(N_HEADS, dtype=I32)[None, :, None, None]
    qrows = jnp.where(head_of_row == head_of_col, q, 0.0).reshape(n_seq, nr, C_HEADS)
    pt = (page_table + page_base).astype(I32)
    page_specs = [pl.BlockSpec((1, C_HEADS, PAGE_SIZE), lambda b, j, pt, u=u: (pt[b, j * n_grp + u], 0, 0))
                  for u in range(n_grp)]
    lg_spec = pl.BlockSpec((1, n_grp, nr, PAGE_SIZE), lambda b, j, pt: (b, j, 0, 0))
    sc_spec = pl.BlockSpec((1, nr, LANES), lambda b, j, pt: (b, 0, 0))
    qr_spec = pl.BlockSpec((1, nr, C_HEADS), lambda b, j, pt: (b, 0, 0))

    logits, scores = pl.pallas_call(
        functools.partial(_sattn_k_kernel, n_grp=n_grp),
        out_shape=(jax.ShapeDtypeStruct((n_seq, n_pages, nr, PAGE_SIZE), F32),
                   jax.ShapeDtypeStruct((n_seq, nr, LANES), F32)),
        grid_spec=pltpu.PrefetchScalarGridSpec(
            num_scalar_prefetch=1, grid=(n_seq, n_steps),
            in_specs=[qr_spec] + page_specs,
            out_specs=(lg_spec, sc_spec)),
        compiler_params=_cparams(2), name="moba_sample_k",
    )(pt, qrows, *([cache_kt] * n_grp))

    rb = row0 // n_new
    return pl.pallas_call(
        functools.partial(_sattn_v_kernel, n_new=n_new, n_grp=n_grp, n_blocks=n_blocks),
        out_shape=jax.ShapeDtypeStruct((n_seq * n_new, C_HEADS), F32),
        grid_spec=pltpu.PrefetchScalarGridSpec(
            num_scalar_prefetch=1, grid=(n_seq, n_steps),
            in_specs=[lg_spec, sc_spec] + page_specs + [
                qr_spec,
                pl.BlockSpec((n_new, C_HEADS), lambda b, j, pt: (rb + b, 1)),
                pl.BlockSpec((n_new, C_HEADS), lambda b, j, pt: (rb + b, 2))],
            out_specs=pl.BlockSpec((n_new, C_HEADS), lambda b, j, pt: (b, 0)),
            scratch_shapes=[pltpu.VMEM((nr, LANES), F32), pltpu.VMEM((nr, 1), F32),
                            pltpu.VMEM((nr, 1), F32), pltpu.VMEM((nr, C_HEADS), F32)]),
        compiler_params=_cparams(2), name="moba_sample_v",
    )(pt, logits, scores, *([cache_vt] * n_grp), qrows, qkv, qkv)


def _rwkv_prep_kernel(*refs, has_vres):
    if has_vres:
        (pr_ref, prev_ref, mu_ref, w0_ref, a0_ref, wup_ref, aup_ref, gup_ref, kk_ref_, ka_ref, e_ref,
         vf_ref, v0_ref, vdn_ref, vup_ref,
         r_o, w_o, lw_o, k_o, v_o, kk_o, b_o, g_o) = refs
    else:
        (pr_ref, prev_ref, mu_ref, w0_ref, a0_ref, wup_ref, aup_ref, gup_ref, kk_ref_, ka_ref, e_ref,
         r_o, w_o, lw_o, k_o, v_o, kk_o, b_o, g_o) = refs
    pr = pr_ref[...]
    xm = pr + (prev_ref[...] - pr) * mu_ref[...]
    c = C_HEADS
    r, k, v, lora = xm[:, :c], xm[:, c:2 * c], xm[:, 2 * c:3 * c], xm[:, 3 * c:]
    z = w0_ref[...] + _dot_ref(jnp.tanh(lora), wup_ref[...])
    w_log = -(jnp.maximum(-z, 0.0) + jnp.log(1.0 + jnp.exp(-jnp.abs(z)))) - 0.5
    log_decay = -jnp.exp(w_log)
    decay = jnp.exp(log_decay)
    a = _sigmoid(a0_ref[...] + _dot_ref(lora, aup_ref[...]))
    g = _dot_ref(_sigmoid(lora), gup_ref[...])
    if has_vres:
        gate = _sigmoid(v0_ref[...] + _dot_ref(_dot_ref(v, vdn_ref[...]), vup_ref[...]))
        v = v + (vf_ref[...] - v) * gate
    kk = k * kk_ref_[...]
    ss = _head_segsum(kk * kk, e_ref)
    kk = kk * lax.rsqrt(jnp.maximum(ss, 1e-24))
    r_o[...] = r
    w_o[...] = decay
    lw_o[...] = log_decay
    k_o[...] = k * (1.0 + (a - 1.0) * ka_ref[...])
    v_o[...] = v
    kk_o[...] = kk
    b_o[...] = kk * a
    g_o[...] = g


def _rwkv_prep(pr, prev, vecs, mats, e128, vres):
    t = pr.shape[0]
    tm = 128
    c = C_HEADS
    row = lambda w: pl.BlockSpec((tm, w), lambda i: (i, 0))
    vec = lambda w: pl.BlockSpec((1, w), lambda i: (0, 0))
    full = lambda a: pl.BlockSpec(a.shape, lambda i: (0, 0))
    mu, w0, a0, k_k, k_a = vecs
    wup, aup, gup = mats
    args = [pr, prev, mu, w0, a0, wup, aup, gup, k_k, k_a, e128]
    specs = [row(C_SHIFT_PAD), row(C_SHIFT_PAD), vec(C_SHIFT_PAD), vec(c), vec(c),
             full(wup), full(aup), full(gup), vec(c), vec(c), full(e128)]
    if vres is not None:
        v_first, v0, vdn, vup = vres
        args += [v_first, v0, vdn, vup]
        specs += [row(c), vec(c), full(vdn), full(vup)]
    return pl.pallas_call(
        functools.partial(_rwkv_prep_kernel, has_vres=vres is not None),
        out_shape=tuple(jax.ShapeDtypeStruct((t, c), F32) for _ in range(8)),
        grid=(t // tm,),
        in_specs=specs,
        out_specs=tuple(row(c) for _ in range(8)),
        compiler_params=_cparams(1), name="rwkv_prep",
    )(*args)


def _rwkv_rec_kernel(r_ref, w_ref, k_ref, v_ref, kk_ref, b_ref, s0_ref, y_ref, st_ref, state):
    nb, tc = r_ref.shape[:2]
    c = pl.program_id(1)

    @pl.when(c == 0)
    def _():
        state[...] = s0_ref[...]

    lane = lax.broadcasted_iota(I32, (HEAD_DIM, LANES), 1)
    row = lax.broadcasted_iota(I32, (HEAD_DIM, LANES), 0)
    low = lane < HEAD_DIM
    eye = jnp.where(low, lane, lane - HEAD_DIM) == row

    def seg(x):
        e = jnp.sum(jnp.where(low, x, 0.0), axis=1, keepdims=True)
        o = jnp.sum(jnp.where(low, 0.0, x), axis=1, keepdims=True)
        return jnp.where(low, e, o)

    def step(t, carry):
        for bi in range(nb):
            r_t, w_t, k_t, v_t, kk_t, b_t = (ref[bi, t] for ref in
                                             (r_ref, w_ref, k_ref, v_ref, kk_ref, b_ref))
            y_rows = []
            for p in range(N_PAIRS):
                s_old = state[bi, p]
                sa = seg(s_old * (-kk_t[p:p + 1, :]))
                v_col = seg(jnp.where(eye, v_t[p:p + 1, :], 0.0))
                s_new = s_old * w_t[p:p + 1, :] + sa * b_t[p:p + 1, :] + v_col * k_t[p:p + 1, :]
                state[bi, p] = s_new
                y_b = seg(s_new * r_t[p:p + 1, :])
                y_rows.append(jnp.sum(jnp.where(eye, y_b, 0.0), axis=0, keepdims=True))
            y_ref[bi, t] = jnp.concatenate(y_rows, axis=0)
        return carry

    lax.fori_loop(0, tc, step, 0)

    @pl.when(c == pl.num_programs(1) - 1)
    def _():
        st_ref[...] = state[...]


def _rwkv_rec(seqs, s0, nb, tc):
    b, t = seqs[0].shape[:2]
    blk = pl.BlockSpec((nb, tc, N_PAIRS, LANES), lambda g, i: (g, i, 0, 0))
    st = pl.BlockSpec((nb, N_PAIRS, HEAD_DIM, LANES), lambda g, i: (g, 0, 0, 0))
    return pl.pallas_call(
        _rwkv_rec_kernel,
        out_shape=(jax.ShapeDtypeStruct((b, t, N_PAIRS, LANES), F32),
                   jax.ShapeDtypeStruct((b, N_PAIRS, HEAD_DIM, LANES), F32)),
        grid=(b // nb, t // tc),
        in_specs=[blk] * 6 + [st],
        out_specs=(blk, st),
        scratch_shapes=[pltpu.VMEM((nb, N_PAIRS, HEAD_DIM, LANES), F32)],
        compiler_params=_cparams(2), name="rwkv_rec",
    )(*seqs, s0)


def _sp(x):
    return _split2(x)


def _mm3(a, b, mode="nn"):
    if mode == "nt":
        d = _dot_nt
    elif mode == "tn":
        d = lambda x, y: lax.dot_general(x, y, (((0,), (0,)), ((), ())), preferred_element_type=F32)
    else:
        d = _dot
    return d(a[0], b[0]) + (d(a[0], b[1]) + d(a[1], b[0]))


def _rwkv_chunk_kernel(r_ref, lw_ref, k_ref, v_ref, kk_ref, b_ref, y_ref, st_ref, hbd, *, pairs):
    c = pl.program_id(2)
    n_c = r_ref.shape[0]
    n2 = 2 * n_c
    log_c = _log2(n_c)

    @pl.when(c == 0)
    def _():
        hbd[...] = jnp.zeros(hbd.shape, F32)

    lane = lax.broadcasted_iota(I32, (n_c, LANES), 1)
    low = lane < HEAD_DIM
    i = lax.broadcasted_iota(I32, (n2, n2), 0)
    j = lax.broadcasted_iota(I32, (n2, n2), 1)
    same = (i >> log_c) == (j >> log_c)
    ti = i & (n_c - 1)
    tj = j & (n_c - 1)
    m_stril = jnp.where(same & (tj < ti), 1.0, 0.0)
    m_tril = jnp.where(same & (tj <= ti), 1.0, 0.0)
    eye = jnp.where(i == j, 1.0, 0.0)
    m_blk = {s: jnp.where((i >> s) == (j >> s), 1.0, 0.0) for s in range(3, log_c + 1)}
    ci = lax.broadcasted_iota(I32, (n_c, n_c), 0)
    cj = lax.broadcasted_iota(I32, (n_c, n_c), 1)
    tri_c = jnp.where(cj <= ci, 1.0, 0.0).astype(BF16)

    def stack(x):
        return jnp.concatenate([jnp.where(low, x, 0.0), jnp.where(low, 0.0, x)], axis=0)

    def each(fn, *cols):
        return [fn(*args) for args in zip(*cols)]

    sls = [slice(q * LANES, (q + 1) * LANES) for q in range(pairs)]
    lw = [lw_ref[:, sl] for sl in sls]
    cum = each(lambda x: _dot_exact_lhs(tri_c, x), lw)
    cum_c = each(lambda x: x[n_c - 1:n_c, :], cum)
    e_neg = each(lambda x: jnp.exp(-x), cum)
    e_tail = each(lambda x, xc: jnp.exp(xc - x), cum, cum_c)
    kk = [kk_ref[:, sl] for sl in sls]
    b = [b_ref[:, sl] for sl in sls]
    k = [k_ref[:, sl] for sl in sls]
    a_s = each(lambda x, cu, l: _sp(stack(-x * jnp.exp(cu - l))), kk, cum, lw)
    b_s = each(lambda x, e: _sp(stack(x * e)), b, e_neg)
    k_s = each(lambda x, e: _sp(stack(x * e)), k, e_neg)
    r_st = [stack(r_ref[:, sl] * jnp.exp(cu)) for sl, cu in zip(sls, cum)]
    r_s = each(_sp, r_st)
    bh_s = each(lambda x, e: _sp(stack(x * e)), b, e_tail)
    kh_s = each(lambda x, e: _sp(stack(x * e)), k, e_tail)
    v_s = [_sp(stack(v_ref[:, sl])) for sl in sls]

    l_ab = each(lambda x, y: m_stril * _mm3(x, y, "nt"), a_s, b_s)
    l_ak = each(lambda x, y: _sp(m_stril * _mm3(x, y, "nt")), a_s, k_s)
    m_rb = each(lambda x, y: _sp(m_tril * _mm3(x, y, "nt")), r_s, b_s)
    m_rk = each(lambda x, y: _sp(m_tril * _mm3(x, y, "nt")), r_s, k_s)

    d1 = each(lambda x: x * m_blk[3], l_ab)
    d1s = each(_sp, d1)
    d2 = each(lambda x: _mm3(x, x), d1s)
    d2s = each(_sp, d2)
    d4 = each(lambda x: _mm3(x, x), d2s)
    i12 = each(lambda x, y: _sp(_mm3(_sp(eye + x), _sp(eye + y))), d1, d2)
    inv = each(lambda x, y: _mm3(x, _sp(eye + y)), i12, d4)
    for s in range(4, log_c + 1):
        l_m = each(lambda x: _sp(x * (m_blk[s] - m_blk[s - 1])), l_ab)
        inv_s = each(_sp, inv)
        t_m = each(lambda x, y: _sp(_mm3(x, y)), inv_s, l_m)
        inv = each(lambda x, y, z: x + _mm3(y, z), inv, t_m, inv_s)
    inv_s = each(_sp, inv)
    w1 = each(lambda x, y: _sp(_mm3(x, y)), inv_s, a_s)
    lv = each(lambda x, y: _sp(_mm3(x, y)), l_ak, v_s)
    u0 = each(lambda x, y: _sp(_mm3(x, y)), inv_s, lv)
    y1 = each(lambda x, y, z: x + _mm3(y, z), r_st, m_rb, w1)
    y0 = each(lambda x, y, z, u: _mm3(x, y) + _mm3(z, u), m_rb, u0, m_rk, v_s)
    g = each(lambda xc, x, y: _sp(eye * jnp.exp(xc) + _mm3(x, y, "tn")), cum_c, bh_s, w1)
    h_add = each(lambda x, y, z, u: _mm3(x, y, "tn") + _mm3(z, u, "tn"), bh_s, u0, kh_s, v_s)
    y1c = each(lambda x: _sp(x[:n_c] + x[n_c:]), y1)
    y0c = each(lambda x: x[:n_c] + x[n_c:], y0)

    h_s = [_sp(hbd[q]) for q in range(pairs)]
    for q in range(pairs):
        y_ref[:, sls[q]] = _mm3(y1c[q], h_s[q]) + y0c[q]
    for q in range(pairs):
        hbd[q] = _mm3(g[q], h_s[q]) + h_add[q]

    @pl.when(c == pl.num_programs(2) - 1)
    def _():
        st_ref[0] = hbd[...]


def _rwkv_chunked(seqs, n_seq, seq_len, chunk, pairs):
    n_chunks = seq_len // chunk
    n_grp = N_PAIRS // pairs
    blk = pl.BlockSpec((chunk, pairs * LANES), lambda b, p, c: (b * n_chunks + c, p))
    return pl.pallas_call(
        functools.partial(_rwkv_chunk_kernel, pairs=pairs),
        out_shape=(jax.ShapeDtypeStruct((n_seq * seq_len, C_HEADS), F32),
                   jax.ShapeDtypeStruct((n_seq * n_grp, pairs, LANES, LANES), F32)),
        grid=(n_seq, n_grp, n_chunks),
        in_specs=[blk] * 6,
        out_specs=(blk, pl.BlockSpec((1, pairs, LANES, LANES), lambda b, p, c: (b * n_grp + p, 0, 0, 0))),
        scratch_shapes=[pltpu.VMEM((pairs, LANES, LANES), F32)],
        compiler_params=_cparams(3), name="rwkv_chunked",
    )(*seqs)


def _unpack_hbd(h, n_seq):
    h = h.reshape(n_seq, N_PAIRS, 2, HEAD_DIM, 2, HEAD_DIM)
    diag = jnp.stack([h[:, :, 0, :, 0, :], h[:, :, 1, :, 1, :]], axis=2)
    return jnp.transpose(diag, (0, 1, 2, 4, 3)).reshape(n_seq, N_HEADS, HEAD_DIM, HEAD_DIM)


def _rwkv_post_kernel(y_ref, r_ref, k_ref, v_ref, g_ref, lnw_ref, lnb_ref, rk_ref, e_ref, o_ref):
    y = y_ref[...]
    inv = 1.0 / HEAD_DIM
    mean = _head_segsum(y, e_ref) * inv
    yc = y - mean
    var = _head_segsum(yc * yc, e_ref) * inv
    yn = yc * lax.rsqrt(var + LN_X_EPS) * lnw_ref[...] + lnb_ref[...]
    bonus = _head_segsum(r_ref[...] * k_ref[...] * rk_ref[...], e_ref) * v_ref[...]
    o_ref[...] = ((yn + bonus) * g_ref[...]).astype(o_ref.dtype)


def _rwkv_post(y, r, k, v, g, ln_w, ln_b, r_k, e128):
    t, c = y.shape
    row = pl.BlockSpec((ROW_TILE, c), lambda i: (i, 0))
    vec = pl.BlockSpec((1, c), lambda i: (0, 0))
    return pl.pallas_call(
        _rwkv_post_kernel,
        out_shape=jax.ShapeDtypeStruct((t, c), BF16),
        grid=(t // ROW_TILE,),
        in_specs=[row] * 5 + [vec] * 3 + [pl.BlockSpec(e128.shape, lambda i: (0, 0))],
        out_specs=row,
        compiler_params=_cparams(1), name="rwkv_post",
    )(y, r, k, v, g, ln_w, ln_b, r_k, e128)


def _pack_state(s):
    b = s.shape[0]
    s = s.reshape(b, N_PAIRS, 2, HEAD_DIM, HEAD_DIM)
    return jnp.transpose(s, (0, 1, 3, 2, 4)).reshape(b, N_PAIRS, HEAD_DIM, LANES)


def _unpack_state(s):
    b = s.shape[0]
    s = s.reshape(b, N_PAIRS, HEAD_DIM, 2, HEAD_DIM)
    return jnp.transpose(s, (0, 1, 3, 2, 4)).reshape(b, N_HEADS, HEAD_DIM, HEAD_DIM)


def _gather_rows_kernel(idx_ref, nrows_ref, src_ref, o_ref, sem):
    rows = o_ref.shape[0]
    base = pl.program_id(0) * rows

    def copy(r):
        return pltpu.make_async_copy(src_ref.at[pl.ds(idx_ref[base + r], 1), :],
                                     o_ref.at[pl.ds(r, 1), :], sem)

    def start(r, carry):
        copy(r).start()
        return carry

    def wait(r, carry):
        copy(r).wait()
        return carry

    @pl.when(base < nrows_ref[0])
    def _():
        lax.fori_loop(0, rows, start, 0, unroll=8)
        lax.fori_loop(0, rows, wait, 0, unroll=8)

    @pl.when(base >= nrows_ref[0])
    def _():
        o_ref[...] = jnp.zeros(o_ref.shape, o_ref.dtype)


def _gather_rows(src, idx, n_valid, rows_per_step):
    n = idx.shape[0]
    w = src.shape[1]
    return pl.pallas_call(
        _gather_rows_kernel,
        out_shape=jax.ShapeDtypeStruct((n, w), src.dtype),
        grid_spec=pltpu.PrefetchScalarGridSpec(
            num_scalar_prefetch=2, grid=(n // rows_per_step,),
            in_specs=[pl.BlockSpec(memory_space=pl.ANY)],
            out_specs=pl.BlockSpec((rows_per_step, w), lambda i, idx, nv: (i, 0)),
            scratch_shapes=[pltpu.SemaphoreType.DMA]),
        compiler_params=_cparams(1), name="gather_rows",
    )(idx, n_valid, src)


def _expert_kernel(te_ref, nu_ref, x_ref, w1_ref, b1_ref, w2_ref, b2_ref, sel_ref, o_ref, xb):
    t = pl.program_id(0)
    j = pl.program_id(1)
    nf = pl.num_programs(1)

    @pl.when(t < nu_ref[0])
    def _():
        @pl.when(j == 0)
        def _():
            xb[...] = x_ref[...].astype(BF16)

        hh = _dot(xb[...], w1_ref[0].astype(BF16)) + b1_ref[0]
        width = hh.shape[1]
        nxt = pltpu.roll(hh, width - 1, 1)
        glu = jnp.minimum(hh, SWIGLU_LIMIT)
        lin = jnp.clip(nxt, -SWIGLU_LIMIT, SWIGLU_LIMIT)
        act = glu * _sigmoid(SWIGLU_ALPHA * glu) * (lin + 1.0)
        act = _dot(act.astype(BF16), sel_ref[...]).astype(BF16)
        part = _dot(act, w2_ref[0].astype(BF16))

        @pl.when(j == 0)
        def _():
            o_ref[...] = part + b2_ref[0]

        @pl.when(j > 0)
        def _():
            o_ref[...] = o_ref[...] + part

    @pl.when((t >= nu_ref[0]) & (j == 0))
    def _():
        o_ref[...] = jnp.zeros(o_ref.shape, F32)


def _expert_mlp(xs, tile_expert, n_used, w1, b1, w2, b2):
    n_rows = xs.shape[0]
    tm, bf = EXPERT_TILE, EXPERT_FBLOCK
    n_tiles = n_rows // tm
    nf = D_EXPERT // bf
    sel = (jnp.arange(2 * bf, dtype=I32)[:, None] == 2 * jnp.arange(bf, dtype=I32)[None, :]).astype(BF16)
    last = lambda t, nu: jnp.minimum(t, nu[0] - 1)
    return pl.pallas_call(
        _expert_kernel,
        out_shape=jax.ShapeDtypeStruct((n_rows, D_MODEL), F32),
        grid_spec=pltpu.PrefetchScalarGridSpec(
            num_scalar_prefetch=2, grid=(n_tiles, nf),
            in_specs=[pl.BlockSpec((tm, D_MODEL), lambda t, j, te, nu: (last(t, nu), 0)),
                      pl.BlockSpec((1, D_MODEL, 2 * bf), lambda t, j, te, nu: (te[t], 0, j)),
                      pl.BlockSpec((1, 1, 2 * bf), lambda t, j, te, nu: (te[t], 0, j)),
                      pl.BlockSpec((1, bf, D_MODEL), lambda t, j, te, nu: (te[t], j, 0)),
                      pl.BlockSpec((1, 1, D_MODEL), lambda t, j, te, nu: (te[t], 0, 0)),
                      pl.BlockSpec((2 * bf, bf), lambda t, j, te, nu: (0, 0))],
            out_specs=pl.BlockSpec((tm, D_MODEL), lambda t, j, te, nu: (t, 0)),
            scratch_shapes=[pltpu.VMEM((tm, D_MODEL), BF16)]),
        compiler_params=_cparams(2), name="expert_mlp",
    )(tile_expert, n_used, xs, w1.reshape(-1, D_MODEL, 2 * D_EXPERT), b1.reshape(-1, 1, 2 * D_EXPERT),
      w2.reshape(-1, D_EXPERT, D_MODEL), b2.reshape(-1, 1, D_MODEL), sel)


def _combine_kernel(slot_ref, ys_ref, x_ref, route_ref, gt_ref, o_ref, ybuf, sem, *, grp):
    rows = x_ref.shape[0]
    base = pl.program_id(0) * rows

    def copy(r, j):
        return pltpu.make_async_copy(ys_ref.at[pl.ds(slot_ref[(base + r) * TOP_K + j], 1), :],
                                     ybuf.at[j, pl.ds(r, 1), :], sem)

    def start(r, carry):
        for j in range(TOP_K):
            copy(r, j).start()
        return carry

    def wait(r, carry):
        for j in range(TOP_K):
            copy(r, j).wait()
        return carry

    lax.fori_loop(0, rows, start, 0, unroll=4)
    lax.fori_loop(0, rows, wait, 0, unroll=4)
    route = route_ref[...]
    moe = jnp.zeros(x_ref.shape, F32)
    for j in range(TOP_K):
        moe = moe + route[:, TOP_K + j:TOP_K + j + 1] * ybuf[j]
    oh = _seq_onehot(base, rows, grp)
    o_ref[...] = x_ref[...] + _dot_exact_lhs(oh, gt_ref[...]) * moe


def _combine(x, ys, slots, route, mod, k_gt, grp):
    t = x.shape[0]
    tm = 128
    return pl.pallas_call(
        functools.partial(_combine_kernel, grp=grp),
        out_shape=jax.ShapeDtypeStruct((t, D_MODEL), F32),
        grid_spec=pltpu.PrefetchScalarGridSpec(
            num_scalar_prefetch=1, grid=(t // tm,),
            in_specs=[pl.BlockSpec(memory_space=pl.ANY),
                      pl.BlockSpec((tm, D_MODEL), lambda i, s: (i, 0)),
                      pl.BlockSpec((tm, LANES), lambda i, s: (i, 0)),
                      pl.BlockSpec((SEQ_TABLE_ROWS, D_MODEL), lambda i, s, k=k_gt: (0, k))],
            out_specs=pl.BlockSpec((tm, D_MODEL), lambda i, s: (i, 0)),
            scratch_shapes=[pltpu.VMEM((TOP_K, tm, D_MODEL), F32), pltpu.SemaphoreType.DMA]),
        compiler_params=_cparams(1), name="moe_combine",
    )(slots, ys, x, route, mod)


def _moe_layer(x, g, mod, wr_pad, br_pad, w1, b1, w2, b2, layer, grp):
    t = x.shape[0]
    h2, route, counts = _rms_router(x, g, mod, 4, 3, wr_pad, br_pad, grp)
    tm = EXPERT_TILE
    n_tiles = (t * TOP_K) // tm + N_EXPERTS
    cnt = counts[0, :N_EXPERTS].astype(I32)
    tiles_per = (cnt + tm - 1) // tm
    tile_end = jnp.cumsum(tiles_per)
    group_start = (tile_end - tiles_per) * tm
    idx = route[:, :TOP_K].astype(I32)
    pos = route[:, 2 * TOP_K:3 * TOP_K].astype(I32)
    slots = (group_start[idx] + pos).reshape(-1)
    token_of_pair = jnp.repeat(jnp.arange(t, dtype=I32), TOP_K)
    src = jnp.zeros((n_tiles * tm,), I32).at[slots].set(token_of_pair)
    n_used = tile_end[-1:].astype(I32)
    tile_expert = jnp.searchsorted(tile_end, jnp.arange(n_tiles, dtype=I32), side="right").astype(I32)
    tile_expert = jnp.minimum(tile_expert, N_EXPERTS - 1)
    tile_expert = jnp.where(jnp.arange(n_tiles) < n_used[0], tile_expert,
                            tile_expert[jnp.maximum(n_used[0] - 1, 0)])
    xs = _gather_rows(h2, src, n_used * tm, 256)
    ys = _expert_mlp(xs, tile_expert + layer * N_EXPERTS, n_used, w1, b1, w2, b2)
    return _combine(x, ys, slots, route, mod, 5, grp)


def _log2(n):
    s = n.bit_length() - 1
    assert (1 << s) == n, "sequence lengths must be powers of two"
    return s


def kernel(x_prompt, x_sample, cache_k, cache_v, page_table, state_shift, state_wkv, c_prompt, c_sample,
           w_ada, b_ada, norm_mix, norm_ffn, w_in, w_out, rw_mu, rw_w0, rw_w_up, rw_a0, rw_a_up,
           rw_g_up, rw_k_k, rw_k_a, rw_r_k, rw_ln_w, rw_ln_b, rw_v0, rw_v_down, rw_v_up,
           w_router, b_router, w_mlp1, b_mlp1, w_mlp2, b_mlp2, norm_final):
    bp, tp, d = x_prompt.shape
    bs, ts, _ = x_sample.shape
    depth = w_ada.shape[0]
    n_pool = cache_k.shape[1]
    np_rows, ns_rows = bp * tp, bs * ts
    t_all = np_rows + ns_rows
    assert d == D_MODEL and t_all % ROW_TILE == 0 and np_rows % ROW_TILE == 0
    assert bp + bs <= SEQ_TABLE_ROWS and tp % MOBA_BLOCK == 0 and ts <= SUBLANES
    grp = (np_rows, _log2(tp), _log2(ts), bp)
    c = C_HEADS

    x = jnp.concatenate([x_prompt.reshape(np_rows, d), x_sample.reshape(ns_rows, d)], axis=0)
    c_pad = jnp.zeros((SEQ_TABLE_ROWS, d), F32).at[:bp + bs].set(jnp.concatenate([c_prompt, c_sample], axis=0))
    mod = _ada(c_pad, w_ada, b_ada)

    cache_kt = jnp.transpose(cache_k, (0, 1, 3, 4, 2)).reshape(depth * n_pool, c, PAGE_SIZE)
    cache_vt = jnp.transpose(cache_v, (0, 1, 3, 4, 2)).reshape(depth * n_pool, c, PAGE_SIZE)
    e128 = (jnp.arange(LANES)[:, None] // HEAD_DIM == jnp.arange(LANES)[None, :] // HEAD_DIM).astype(BF16)

    def lora_pad(w, start):
        return jnp.zeros((C_LORA_PAD, c), F32).at[start:start + w.shape[0]].set(w)

    k_out, v_out, shift_p, shift_s, wkv_p, wkv_s = [], [], [], [], [], []
    v_first = None
    for l in range(depth):
        h = _rms_mod(x, norm_mix[l], mod[l], 1, 0, grp)
        w_qkv = w_in[l][:, :3 * c].astype(BF16)
        w_pr = jnp.zeros((d, C_SHIFT_PAD), BF16).at[:, :C_SHIFT].set(w_in[l][:, 3 * c:].astype(BF16))
        qkv = _matmul(h, w_qkv, 768, 1024)
        pr = _matmul(h, w_pr, 768, 512)

        att_p = _prompt_attention(qkv, bp, tp)
        att_s = _sample_attention(qkv, np_rows, bs, ts, cache_kt, cache_vt, page_table, l * n_pool)
        att = jnp.concatenate([att_p, att_s.astype(BF16)], axis=0)

        pr_p = pr[:np_rows].reshape(bp, tp, C_SHIFT_PAD)
        pr_s = pr[np_rows:].reshape(bs, ts, C_SHIFT_PAD)
        shift0_s = jnp.zeros((bs, 1, C_SHIFT_PAD), F32).at[:, 0, :C_SHIFT].set(state_shift[l])
        prev = jnp.concatenate([
            jnp.concatenate([jnp.zeros((bp, 1, C_SHIFT_PAD), F32), pr_p[:, :-1]], axis=1).reshape(np_rows, -1),
            jnp.concatenate([shift0_s, pr_s[:, :-1]], axis=1).reshape(ns_rows, -1)], axis=0)
        mu = jnp.zeros((1, C_SHIFT_PAD), F32).at[0, :C_SHIFT].set(rw_mu[l])
        vecs = (mu, rw_w0[l].reshape(1, c), rw_a0[l].reshape(1, c), rw_k_k[l].reshape(1, c),
                rw_k_a[l].reshape(1, c))
        mats = (lora_pad(rw_w_up[l], 0), lora_pad(rw_a_up[l], D_DECAY_LORA),
                lora_pad(rw_g_up[l], D_DECAY_LORA + D_AAA_LORA))
        vres = None
        if l > 0:
            vdn = jnp.zeros((c, LANES), F32).at[:, :D_MV_LORA].set(rw_v_down[l - 1])
            vup = jnp.zeros((LANES, c), F32).at[:D_MV_LORA].set(rw_v_up[l - 1])
            vres = (v_first, rw_v0[l - 1].reshape(1, c), vdn, vup)
        r, w, lw, k, v, kk, b, g = _rwkv_prep(pr, prev, vecs, mats, e128, vres)
        if l == 0:
            v_first = v
        y_p, h_p = _rwkv_chunked((r, lw, k, v, kk, b), bp, tp, RWKV_CHUNK, RWKV_PAIRS_PER_STEP)
        seqs_s = [a[np_rows:].reshape(bs, ts, N_PAIRS, LANES) for a in (r, w, k, v, kk, b)]
        y_s, st_s = _rwkv_rec(seqs_s, _pack_state(state_wkv[l]), 4, ts)
        y = jnp.concatenate([y_p, y_s.reshape(ns_rows, c)], axis=0)
        rw = _rwkv_post(y, r, k, v, g, rw_ln_w[l].reshape(1, c), rw_ln_b[l].reshape(1, c),
                        rw_r_k[l].reshape(1, c), e128)

        x = _mix_out(att, rw, w_out[l].astype(BF16), x, mod[l], 2, grp)

        wr_pad = jnp.zeros((d, LANES), F32).at[:, :N_EXPERTS].set(w_router[l])
        br_pad = jnp.full((1, LANES), NEG_BIG, F32).at[0, :N_EXPERTS].set(b_router[l])
        x = _moe_layer(x, norm_ffn[l], mod[l], wr_pad, br_pad, w_mlp1, b_mlp1, w_mlp2, b_mlp2, l, grp)

        k_out.append(qkv[:, c:2 * c])
        v_out.append(qkv[:, 2 * c:3 * c])
        shift_p.append(pr_p[:, -1, :C_SHIFT])
        shift_s.append(pr_s[:, -1, :C_SHIFT])
        wkv_p.append(_unpack_hbd(h_p.reshape(bp * N_PAIRS, LANES, LANES), bp))
        wkv_s.append(_unpack_state(st_s))

    y = _final_norm(x, norm_final)
    k_all, v_all = jnp.stack(k_out), jnp.stack(v_out)
    pages = tp // PAGE_SIZE
    return (y[:np_rows].reshape(bp, tp, d), y[np_rows:].reshape(bs, ts, d),
            k_all[:, :np_rows].reshape(depth, bp, pages, PAGE_SIZE, N_HEADS, HEAD_DIM),
            v_all[:, :np_rows].reshape(depth, bp, pages, PAGE_SIZE, N_HEADS, HEAD_DIM),
            jnp.stack(shift_p), jnp.stack(wkv_p),
            k_all[:, np_rows:].reshape(depth, bs, ts, N_HEADS, HEAD_DIM),
            v_all[:, np_rows:].reshape(depth, bs, ts, N_HEADS, HEAD_DIM),
            jnp.stack(shift_s), jnp.stack(wkv_s))
```

```python
import functools

import jax
import jax.numpy as jnp
from jax import lax
from jax.experimental import pallas as pl
from jax.experimental.pallas import tpu as pltpu

F32, BF16, I32 = jnp.float32, jnp.bfloat16, jnp.int32

LANES = 128
SUBLANES = 8
VMEM_LIMIT = 50 * 1024 * 1024

D_MODEL = 2048
HEAD_DIM = 64
N_HEADS = 16
C_HEADS = N_HEADS * HEAD_DIM
N_PAIRS = C_HEADS // LANES
MOBA_BLOCK = 256
MOBA_TOPK = 3
PAGE_SIZE = 128
D_DECAY_LORA, D_AAA_LORA, D_GATE_LORA, D_MV_LORA = 64, 64, 160, 32
C_SHIFT = 3 * C_HEADS + D_DECAY_LORA + D_AAA_LORA + D_GATE_LORA
C_SHIFT_PAD = 3584
C_LORA_PAD = C_SHIFT_PAD - 3 * C_HEADS
LN_X_EPS = 64e-5
RMS_EPS = 1e-5
N_EXPERTS = 32
TOP_K = 4
D_EXPERT = D_MODEL
SWIGLU_LIMIT = 7.0
SWIGLU_ALPHA = 1.702
NEG_BIG = -3.0e38

ROW_TILE = 256
SEQ_TABLE_ROWS = 128
EXPERT_TILE = 512
EXPERT_FBLOCK = 256
EXPERT_SUBTILES = 2
RWKV_CHUNK = 64
RWKV_PAIRS_PER_STEP = 8
SAMPLE_PAGES_PER_STEP = 8
PROMPT_PAIRS_PER_STEP = 4


def _cparams(n_axes):
    return pltpu.CompilerParams(dimension_semantics=("arbitrary",) * n_axes,
                                vmem_limit_bytes=VMEM_LIMIT)


def _dot(a, b):
    return jnp.dot(a, b, preferred_element_type=F32)


def _dot_nt(a, b):
    return lax.dot_general(a, b, (((1,), (1,)), ((), ())), preferred_element_type=F32)


def _split2(x):
    hi = x.astype(BF16)
    lo = (x - hi.astype(F32)).astype(BF16)
    return hi, lo


def _split3(x):
    hi = x.astype(BF16)
    r = x - hi.astype(F32)
    mid = r.astype(BF16)
    lo = (r - mid.astype(F32)).astype(BF16)
    return hi, mid, lo


def _dot_ref(a, b, nt=False):
    d = _dot_nt if nt else _dot
    return d(a.astype(BF16), b.astype(BF16))


def _dot_exact_lhs(a_bf16, b):
    bh, bm, bl = _split3(b)
    return _dot(a_bf16, bh) + (_dot(a_bf16, bm) + _dot(a_bf16, bl))


def _seq_onehot(row0, rows, grp):
    n_prompt_rows, tp_shift, ts_shift, n_prompt = grp
    r = row0 + lax.broadcasted_iota(I32, (rows, SEQ_TABLE_ROWS), 0)
    lane = lax.broadcasted_iota(I32, (rows, SEQ_TABLE_ROWS), 1)
    sid = jnp.where(r < n_prompt_rows, r >> tp_shift, n_prompt + ((r - n_prompt_rows) >> ts_shift))
    return jnp.where(lane == sid, 1.0, 0.0).astype(BF16)


def _head_segsum(x, e_ref):
    e = e_ref[...]
    outs = []
    for c in range(x.shape[1] // LANES):
        hi, lo = _split2(x[:, c * LANES:(c + 1) * LANES])
        outs.append(_dot(hi, e) + _dot(lo, e))
    return jnp.concatenate(outs, axis=1)


def _sigmoid(x):
    return 1.0 / (1.0 + jnp.exp(-x))


def _ada_kernel(c_ref, w_ref, b_ref, o_ref):
    o_ref[0] = _dot_ref(c_ref[...], w_ref[0]) + b_ref[0]


def _ada(c_pad, w_ada, b_ada):
    depth, d, n = w_ada.shape
    tn = 512
    return pl.pallas_call(
        _ada_kernel,
        out_shape=jax.ShapeDtypeStruct((depth, SEQ_TABLE_ROWS, n), F32),
        grid=(depth, n // tn),
        in_specs=[pl.BlockSpec((SEQ_TABLE_ROWS, d), lambda l, j: (0, 0)),
                  pl.BlockSpec((1, d, tn), lambda l, j: (l, 0, j)),
                  pl.BlockSpec((1, 1, tn), lambda l, j: (l, 0, j))],
        out_specs=pl.BlockSpec((1, SEQ_TABLE_ROWS, tn), lambda l, j: (l, 0, j)),
        compiler_params=_cparams(2), name="ada_mod",
    )(c_pad, w_ada, b_ada.reshape(depth, 1, n))


def _rms_modulated(x_ref, g_ref, sc_ref, sh_ref, grp):
    rows = x_ref.shape[0]
    x = x_ref[...]
    y = x * lax.rsqrt(jnp.mean(x * x, axis=-1, keepdims=True) + RMS_EPS) * g_ref[...]
    oh = _seq_onehot(pl.program_id(0) * rows, rows, grp)
    sc = _dot_exact_lhs(oh, sc_ref[...])
    sh = _dot_exact_lhs(oh, sh_ref[...])
    return y * (1.0 + sc) + sh


def _rms_mod_kernel(x_ref, g_ref, sc_ref, sh_ref, o_ref, *, grp):
    o_ref[...] = _rms_modulated(x_ref, g_ref, sc_ref, sh_ref, grp).astype(o_ref.dtype)


def _mod_spec(k):
    return pl.BlockSpec((SEQ_TABLE_ROWS, D_MODEL), lambda i, k=k: (0, k))


def _rms_mod(x, g, mod, k_sc, k_sh, grp):
    t = x.shape[0]
    return pl.pallas_call(
        functools.partial(_rms_mod_kernel, grp=grp),
        out_shape=jax.ShapeDtypeStruct((t, D_MODEL), BF16),
        grid=(t // ROW_TILE,),
        in_specs=[pl.BlockSpec((ROW_TILE, D_MODEL), lambda i: (i, 0)),
                  pl.BlockSpec((1, D_MODEL), lambda i: (0, 0)),
                  _mod_spec(k_sc), _mod_spec(k_sh)],
        out_specs=pl.BlockSpec((ROW_TILE, D_MODEL), lambda i: (i, 0)),
        compiler_params=_cparams(1), name="rms_mod",
    )(x, g.reshape(1, D_MODEL), mod, mod)


def _rms_router_kernel(x_ref, g_ref, sc_ref, sh_ref, wr_ref, br_ref, h_ref, route_ref, cnt_ref,
                       carry, *, grp):
    i = pl.program_id(0)
    rows = x_ref.shape[0]
    h = _rms_modulated(x_ref, g_ref, sc_ref, sh_ref, grp)
    h_ref[...] = h
    logits = _dot_ref(h, wr_ref[...]) + br_ref[...]
    lane = lax.broadcasted_iota(I32, (rows, LANES), 1)
    vals, idxs = [], []
    multi = jnp.zeros((rows, LANES), F32)
    for _ in range(TOP_K):
        m = jnp.max(logits, axis=1, keepdims=True)
        sel = jnp.min(jnp.where(logits == m, lane, LANES), axis=1, keepdims=True)
        hit = lane == sel
        vals.append(m)
        idxs.append(sel)
        multi = jnp.where(hit, 1.0, multi)
        logits = jnp.where(hit, NEG_BIG, logits)
    es = [jnp.exp(v - vals[0]) for v in vals]
    denom = es[0] + es[1] + es[2] + es[3]

    @pl.when(i == 0)
    def _():
        carry[...] = jnp.zeros_like(carry)

    r_i = lax.broadcasted_iota(I32, (rows, rows), 0)
    c_i = lax.broadcasted_iota(I32, (rows, rows), 1)
    tri = jnp.where(c_i < r_i, 1.0, 0.0).astype(BF16)
    cum = _dot(tri, multi.astype(BF16)) + carry[0:1, :]
    route = jnp.zeros((rows, LANES), F32)
    for j in range(TOP_K):
        pos = jnp.sum(jnp.where(lane == idxs[j], cum, 0.0), axis=1, keepdims=True)
        route = jnp.where(lane == j, idxs[j].astype(F32), route)
        route = jnp.where(lane == TOP_K + j, es[j] / denom, route)
        route = jnp.where(lane == 2 * TOP_K + j, pos, route)
    route_ref[...] = route
    new_cnt = carry[0:1, :] + jnp.sum(multi, axis=0, keepdims=True)
    carry[...] = jnp.broadcast_to(new_cnt, carry.shape)
    cnt_ref[...] = jnp.broadcast_to(new_cnt, cnt_ref.shape)


def _rms_router(x, g, mod, k_sc, k_sh, wr_pad, br_pad, grp):
    t = x.shape[0]
    return pl.pallas_call(
        functools.partial(_rms_router_kernel, grp=grp),
        out_shape=(jax.ShapeDtypeStruct((t, D_MODEL), F32),
                   jax.ShapeDtypeStruct((t, LANES), F32),
                   jax.ShapeDtypeStruct((SUBLANES, LANES), F32)),
        grid=(t // ROW_TILE,),
        in_specs=[pl.BlockSpec((ROW_TILE, D_MODEL), lambda i: (i, 0)),
                  pl.BlockSpec((1, D_MODEL), lambda i: (0, 0)),
                  _mod_spec(k_sc), _mod_spec(k_sh),
                  pl.BlockSpec((D_MODEL, LANES), lambda i: (0, 0)),
                  pl.BlockSpec((1, LANES), lambda i: (0, 0))],
        out_specs=(pl.BlockSpec((ROW_TILE, D_MODEL), lambda i: (i, 0)),
                   pl.BlockSpec((ROW_TILE, LANES), lambda i: (i, 0)),
                   pl.BlockSpec((SUBLANES, LANES), lambda i: (0, 0))),
        scratch_shapes=[pltpu.VMEM((SUBLANES, LANES), F32)],
        compiler_params=_cparams(1), name="rms_router",
    )(x, g.reshape(1, D_MODEL), mod, mod, wr_pad, br_pad)


def _final_norm_kernel(x_ref, g_ref, o_ref):
    x = x_ref[...]
    o_ref[...] = x * lax.rsqrt(jnp.mean(x * x, axis=-1, keepdims=True) + RMS_EPS) * g_ref[...]


def _final_norm(x, g):
    t = x.shape[0]
    return pl.pallas_call(
        _final_norm_kernel,
        out_shape=jax.ShapeDtypeStruct((t, D_MODEL), F32),
        grid=(t // ROW_TILE,),
        in_specs=[pl.BlockSpec((ROW_TILE, D_MODEL), lambda i: (i, 0)),
                  pl.BlockSpec((1, D_MODEL), lambda i: (0, 0))],
        out_specs=pl.BlockSpec((ROW_TILE, D_MODEL), lambda i: (i, 0)),
        compiler_params=_cparams(1), name="final_norm",
    )(x, g.reshape(1, D_MODEL))


def _mm_kernel(a_ref, w_ref, o_ref):
    o_ref[...] = _dot(a_ref[...], w_ref[...])


def _matmul(a, w, tm, tn):
    m, k = a.shape
    n = w.shape[1]
    return pl.pallas_call(
        _mm_kernel,
        out_shape=jax.ShapeDtypeStruct((m, n), F32),
        grid=(n // tn, m // tm),
        in_specs=[pl.BlockSpec((tm, k), lambda j, i: (i, 0)),
                  pl.BlockSpec((k, tn), lambda j, i: (0, j))],
        out_specs=pl.BlockSpec((tm, tn), lambda j, i: (i, j)),
        compiler_params=_cparams(2), name="proj_matmul",
    )(a, w)


def _mix_out_kernel(att_ref, rw_ref, wa_ref, wb_ref, x_ref, gt_ref, o_ref, *, grp):
    rows = x_ref.shape[0]
    mix = _dot(att_ref[...], wa_ref[...]) + _dot(rw_ref[...], wb_ref[...])
    oh = _seq_onehot(pl.program_id(1) * rows, rows, grp)
    o_ref[...] = x_ref[...] + _dot_exact_lhs(oh, gt_ref[...]) * mix


def _mix_out(att, rw, w_out, x, mod, k_gt, grp):
    t = x.shape[0]
    tm, tn = ROW_TILE, 1024
    nb = D_MODEL // tn
    return pl.pallas_call(
        functools.partial(_mix_out_kernel, grp=grp),
        out_shape=jax.ShapeDtypeStruct((t, D_MODEL), F32),
        grid=(nb, t // tm),
        in_specs=[pl.BlockSpec((tm, C_HEADS), lambda j, i: (i, 0)),
                  pl.BlockSpec((tm, C_HEADS), lambda j, i: (i, 0)),
                  pl.BlockSpec((C_HEADS, tn), lambda j, i: (0, j)),
                  pl.BlockSpec((C_HEADS, tn), lambda j, i: (1, j)),
                  pl.BlockSpec((tm, tn), lambda j, i: (i, j)),
                  pl.BlockSpec((SEQ_TABLE_ROWS, tn), lambda j, i, k=k_gt, nb=nb: (0, k * nb + j))],
        out_specs=pl.BlockSpec((tm, tn), lambda j, i: (i, j)),
        compiler_params=_cparams(2), name="mix_out",
    )(att, rw, w_out, w_out, x, mod)


def _pattn_kernel(q_ref, k_ref, v_ref, o_ref, kmean):
    blk = q_ref.shape[0]
    n_blk = k_ref.shape[0] // blk
    n_pair = q_ref.shape[1] // LANES
    n_pad = kmean.shape[1]
    qi = pl.program_id(2)
    lane = lax.broadcasted_iota(I32, (blk, LANES), 1)
    low = lane < HEAD_DIM
    heads = [(pr, e) for pr in range(n_pair) for e in range(2)]

    @pl.when(qi == 0)
    def _():
        for pr in range(n_pair):
            rows = [jnp.sum(k_ref[n * blk:(n + 1) * blk, pr * LANES:(pr + 1) * LANES], axis=0, keepdims=True)
                    * (1.0 / blk) for n in range(n_blk)]
            if n_pad > n_blk:
                rows.append(jnp.zeros((n_pad - n_blk, LANES), F32))
            kmean[pr] = jnp.concatenate(rows, axis=0)

    blk_i = lax.broadcasted_iota(I32, (n_pad, blk), 0)
    past = jnp.where(blk_i < qi, 1.0, 0.0)
    qh, selm = [], []
    for pr, e in heads:
        q = q_ref[:, pr * LANES:(pr + 1) * LANES] * (HEAD_DIM ** -0.5)
        qm = jnp.where(low, q, 0.0) if e == 0 else jnp.where(low, 0.0, q)
        qh.append(qm.astype(BF16))
        s = _dot_nt(kmean[pr].astype(BF16), qh[-1])
        cnt = jnp.zeros((n_pad, blk), F32)
        for m in range(n_blk):
            sm = s[m:m + 1, :]
            beats = (sm > s) | ((sm == s) & (m < blk_i))
            cnt = cnt + jnp.where(beats, 1.0, 0.0) * jnp.where(m < qi, 1.0, 0.0)
        sel_t = jnp.where(cnt < MOBA_TOPK, past, 0.0)
        sel_t = jnp.concatenate([sel_t, jnp.zeros((LANES - n_pad, blk), F32)], axis=0)
        selm.append(jnp.transpose(sel_t))
    row_i = lax.broadcasted_iota(I32, (blk, blk), 0)
    col_i = lax.broadcasted_iota(I32, (blk, blk), 1)
    causal = jnp.where(col_i <= row_i, 1.0, 0.0)

    def body(n, carry):
        off = pl.multiple_of(n * blk, blk)
        own = jnp.where(n == qi, 1.0, 0.0)
        out = []
        for h, (pr, e) in enumerate(heads):
            kb = k_ref[pl.ds(off, blk), pr * LANES:(pr + 1) * LANES].astype(BF16)
            vb = v_ref[pl.ds(off, blk), pr * LANES:(pr + 1) * LANES].astype(BF16)
            m_old, l_old, acc = carry[3 * h:3 * h + 3]
            logits = _dot_nt(qh[h], kb)
            flag = jnp.sum(jnp.where(lane == n, selm[h], 0.0), axis=1, keepdims=True)
            allowed = (own * causal + (1.0 - own) * flag) > 0.5
            masked = jnp.where(allowed, logits, NEG_BIG)
            m_new = jnp.maximum(m_old, jnp.max(masked, axis=1, keepdims=True))
            alpha = jnp.exp(m_old - m_new)
            p = jnp.where(allowed, jnp.exp(masked - m_new), 0.0)
            l_new = l_old * alpha + jnp.sum(p, axis=1, keepdims=True)
            acc = acc * alpha + _dot(p.astype(BF16), vb)
            out += [m_new, l_new, acc]
        return tuple(out)

    init = (jnp.full((blk, 1), NEG_BIG, F32), jnp.zeros((blk, 1), F32),
            jnp.zeros((blk, LANES), F32)) * len(heads)
    res = lax.fori_loop(0, qi + 1, body, init)
    for pr in range(n_pair):
        a, b = 6 * pr, 6 * pr + 3
        o_ref[:, pr * LANES:(pr + 1) * LANES] = jnp.where(
            low, res[a + 2] / res[a + 1], res[b + 2] / res[b + 1]).astype(o_ref.dtype)


def _prompt_attention(qkv, n_seq, seq_len):
    blk = MOBA_BLOCK
    nq = seq_len // blk
    npair = PROMPT_PAIRS_PER_STEP
    w = npair * LANES
    ng = N_PAIRS // npair
    n_pad = -(-nq // SUBLANES) * SUBLANES
    assert n_pad <= LANES
    return pl.pallas_call(
        _pattn_kernel,
        out_shape=jax.ShapeDtypeStruct((n_seq * seq_len, C_HEADS), BF16),
        grid=(n_seq, ng, nq),
        in_specs=[pl.BlockSpec((blk, w), lambda b, p, i: (b * nq + i, p)),
                  pl.BlockSpec((seq_len, w), lambda b, p, i: (b, ng + p)),
                  pl.BlockSpec((seq_len, w), lambda b, p, i: (b, 2 * ng + p))],
        out_specs=pl.BlockSpec((blk, w), lambda b, p, i: (b * nq + i, p)),
        scratch_shapes=[pltpu.VMEM((npair, n_pad, LANES), F32)],
        compiler_params=_cparams(3), name="moba_prompt",
    )(qkv, qkv, qkv)


def _sattn_k_kernel(pt_ref, qr_ref, *refs, n_grp):
    kp_refs, (lg_ref, sc_ref) = refs[:n_grp], refs[n_grp:]
    j = pl.program_id(1)
    qr = qr_ref[0]
    qb = qr.astype(BF16)
    pages_per_block = MOBA_BLOCK // PAGE_SIZE
    blocks_per_step = n_grp // pages_per_block
    lane = lax.broadcasted_iota(I32, sc_ref.shape[1:], 1)

    @pl.when(j == 0)
    def _():
        sc_ref[0] = jnp.zeros(sc_ref.shape[1:], F32)

    kts = [ref[0] for ref in kp_refs]
    for u in range(n_grp):
        lg_ref[0, u] = _dot(qb, kts[u].astype(BF16))
    sc = sc_ref[0]
    for n in range(blocks_per_step):
        ksum = kts[n * pages_per_block]
        for u in range(1, pages_per_block):
            ksum = ksum + kts[n * pages_per_block + u]
        kmean = jnp.sum(ksum, axis=1, keepdims=True) * (1.0 / MOBA_BLOCK)
        s = _dot(qb, jnp.broadcast_to(kmean, (C_HEADS, LANES)).astype(BF16))
        sc = jnp.where(lane == j * blocks_per_step + n, s, sc)
    sc_ref[0] = sc


def _sattn_v_kernel(pt_ref, lg_ref, sc_ref, *refs, n_new, n_grp, n_blocks):
    vp_refs = refs[:n_grp]
    qr_ref, kn_ref, vn_ref, o_ref, selm, m_s, l_s, acc = refs[n_grp:]
    j = pl.program_id(1)
    nr = lg_ref.shape[2]
    lane = lax.broadcasted_iota(I32, (nr, LANES), 1)
    pages_per_block = MOBA_BLOCK // PAGE_SIZE

    @pl.when(j == 0)
    def _():
        s = sc_ref[0]
        cnt = jnp.zeros((nr, LANES), F32)
        for m in range(n_blocks):
            sm = s[:, m:m + 1]
            cnt = cnt + jnp.where((sm > s) | ((sm == s) & (m < lane)), 1.0, 0.0)
        selm[...] = jnp.where((cnt < MOBA_TOPK) & (lane < n_blocks), 1.0, 0.0)
        m_s[...] = jnp.full(m_s.shape, NEG_BIG, F32)
        l_s[...] = jnp.zeros(l_s.shape, F32)
        acc[...] = jnp.zeros(acc.shape, F32)

    def accumulate(logits, allowed, pv_fns):
        masked = [jnp.where(a, x, NEG_BIG) for x, a in zip(logits, allowed)]
        m_old = m_s[...]
        m_new = m_old
        for x in masked:
            m_new = jnp.maximum(m_new, jnp.max(x, axis=1, keepdims=True))
        alpha = jnp.exp(m_old - m_new)
        l_new = l_s[...] * alpha
        pv = acc[...] * alpha
        for x, a, fn in zip(masked, allowed, pv_fns):
            p = jnp.where(a, jnp.exp(x - m_new), 0.0)
            l_new = l_new + jnp.sum(p, axis=1, keepdims=True)
            pv = pv + fn(p.astype(BF16))
        l_s[...] = l_new
        acc[...] = pv
        m_s[...] = m_new

    sel = selm[...]
    flags = [jnp.sum(jnp.where(lane == (j * n_grp + u) // pages_per_block, sel, 0.0), axis=1, keepdims=True)
             for u in range(n_grp)]
    accumulate([lg_ref[0, u] for u in range(n_grp)],
               [jnp.broadcast_to(f, (nr, LANES)) > 0.0 for f in flags],
               [functools.partial(lambda p, ref: _dot_nt(p, ref[0].astype(BF16)), ref=ref) for ref in vp_refs])

    @pl.when(j == pl.num_programs(1) - 1)
    def _():
        pad = jnp.zeros((LANES - n_new, C_HEADS), F32)
        kn = jnp.concatenate([kn_ref[...], pad], axis=0).astype(BF16)
        vn = jnp.concatenate([vn_ref[...], pad], axis=0).astype(BF16)
        logits = _dot_nt(qr_ref[0].astype(BF16), kn)
        row = lax.broadcasted_iota(I32, (nr, LANES), 0)
        accumulate([logits], [(lane <= (row & (n_new - 1))) & (lane < n_new)], [lambda p: _dot(p, vn)])
        out = acc[...] / l_s[...]
        r2 = lax.broadcasted_iota(I32, (nr, C_HEADS), 0)
        c2 = lax.broadcasted_iota(I32, (nr, C_HEADS), 1)
        own = jnp.where((r2 >> _log2(n_new)) == (c2 >> _log2(HEAD_DIM)), out, 0.0)
        o_ref[...] = jnp.sum(own.reshape(N_HEADS, n_new, C_HEADS), axis=0)


def _sample_attention(qkv, row0, n_seq, n_new, cache_kt, cache_vt, page_table, page_base):
    n_pages = page_table.shape[1]
    n_grp = SAMPLE_PAGES_PER_STEP
    n_blocks = (n_pages * PAGE_SIZE) // MOBA_BLOCK
    assert n_pages % n_grp == 0 and n_grp % (MOBA_BLOCK // PAGE_SIZE) == 0 and n_blocks <= LANES
    n_steps = n_pages // n_grp
    nr = N_HEADS * n_new
    q = qkv[row0:row0 + n_seq * n_new, :C_HEADS].reshape(n_seq, 1, n_new, C_HEADS) * (HEAD_DIM ** -0.5)
    head_of_col = (jnp.arange(C_HEADS, dtype=I32) // HEAD_DIM)[None, None, None, :]
    head_of_row = jnp.arange(N_HEADS, dtype=I32)[None, :, None, None]
    qrows = jnp.where(head_of_row == head_of_col, q, 0.0).reshape(n_seq, nr, C_HEADS)
    pt = (page_table + page_base).astype(I32)
    page_specs = [pl.BlockSpec((1, C_HEADS, PAGE_SIZE), lambda b, j, pt, u=u: (pt[b, j * n_grp + u], 0, 0))
                  for u in range(n_grp)]
    lg_spec = pl.BlockSpec((1, n_grp, nr, PAGE_SIZE), lambda b, j, pt: (b, j, 0, 0))
    sc_spec = pl.BlockSpec((1, nr, LANES), lambda b, j, pt: (b, 0, 0))
    qr_spec = pl.BlockSpec((1, nr, C_HEADS), lambda b, j, pt: (b, 0, 0))

    logits, scores = pl.pallas_call(
        functools.partial(_sattn_k_kernel, n_grp=n_grp),
        out_shape=(jax.ShapeDtypeStruct((n_seq, n_pages, nr, PAGE_SIZE), F32),
                   jax.ShapeDtypeStruct((n_seq, nr, LANES), F32)),
        grid_spec=pltpu.PrefetchScalarGridSpec(
            num_scalar_prefetch=1, grid=(n_seq, n_steps),
            in_specs=[qr_spec] + page_specs,
            out_specs=(lg_spec, sc_spec)),
        compiler_params=_cparams(2), name="moba_sample_k",
    )(pt, qrows, *([cache_kt] * n_grp))

    rb = row0 // n_new
    return pl.pallas_call(
        functools.partial(_sattn_v_kernel, n_new=n_new, n_grp=n_grp, n_blocks=n_blocks),
        out_shape=jax.ShapeDtypeStruct((n_seq * n_new, C_HEADS), F32),
        grid_spec=pltpu.PrefetchScalarGridSpec(
            num_scalar_prefetch=1, grid=(n_seq, n_steps),
            in_specs=[lg_spec, sc_spec] + page_specs + [
                qr_spec,
                pl.BlockSpec((n_new, C_HEADS), lambda b, j, pt: (rb + b, 1)),
                pl.BlockSpec((n_new, C_HEADS), lambda b, j, pt: (rb + b, 2))],
            out_specs=pl.BlockSpec((n_new, C_HEADS), lambda b, j, pt: (b, 0)),
            scratch_shapes=[pltpu.VMEM((nr, LANES), F32), pltpu.VMEM((nr, 1), F32),
                            pltpu.VMEM((nr, 1), F32), pltpu.VMEM((nr, C_HEADS), F32)]),
        compiler_params=_cparams(2), name="moba_sample_v",
    )(pt, logits, scores, *([cache_vt] * n_grp), qrows, qkv, qkv)


def _rwkv_prep_kernel(*refs, has_vres):
    if has_vres:
        (pr_ref, prev_ref, mu_ref, w0_ref, a0_ref, wup_ref, aup_ref, gup_ref, kk_ref_, ka_ref, e_ref,
         vf_ref, v0_ref, vdn_ref, vup_ref,
         r_o, w_o, lw_o, k_o, v_o, kk_o, b_o, g_o) = refs
    else:
        (pr_ref, prev_ref, mu_ref, w0_ref, a0_ref, wup_ref, aup_ref, gup_ref, kk_ref_, ka_ref, e_ref,
         r_o, w_o, lw_o, k_o, v_o, kk_o, b_o, g_o) = refs
    pr = pr_ref[...]
    xm = pr + (prev_ref[...] - pr) * mu_ref[...]
    c = C_HEADS
    r, k, v, lora = xm[:, :c], xm[:, c:2 * c], xm[:, 2 * c:3 * c], xm[:, 3 * c:]
    z = w0_ref[...] + _dot_ref(jnp.tanh(lora), wup_ref[...])
    w_log = -(jnp.maximum(-z, 0.0) + jnp.log(1.0 + jnp.exp(-jnp.abs(z)))) - 0.5
    log_decay = -jnp.exp(w_log)
    decay = jnp.exp(log_decay)
    a = _sigmoid(a0_ref[...] + _dot_ref(lora, aup_ref[...]))
    g = _dot_ref(_sigmoid(lora), gup_ref[...])
    if has_vres:
        gate = _sigmoid(v0_ref[...] + _dot_ref(_dot_ref(v, vdn_ref[...]), vup_ref[...]))
        v = v + (vf_ref[...] - v) * gate
    kk = k * kk_ref_[...]
    ss = _head_segsum(kk * kk, e_ref)
    kk = kk * lax.rsqrt(jnp.maximum(ss, 1e-24))
    r_o[...] = r
    w_o[...] = decay
    lw_o[...] = log_decay
    k_o[...] = k * (1.0 + (a - 1.0) * ka_ref[...])
    v_o[...] = v
    kk_o[...] = kk
    b_o[...] = kk * a
    g_o[...] = g


def _rwkv_prep(pr, prev, vecs, mats, e128, vres):
    t = pr.shape[0]
    tm = 128
    c = C_HEADS
    row = lambda w: pl.BlockSpec((tm, w), lambda i: (i, 0))
    vec = lambda w: pl.BlockSpec((1, w), lambda i: (0, 0))
    full = lambda a: pl.BlockSpec(a.shape, lambda i: (0, 0))
    mu, w0, a0, k_k, k_a = vecs
    wup, aup, gup = mats
    args = [pr, prev, mu, w0, a0, wup, aup, gup, k_k, k_a, e128]
    specs = [row(C_SHIFT_PAD), row(C_SHIFT_PAD), vec(C_SHIFT_PAD), vec(c), vec(c),
             full(wup), full(aup), full(gup), vec(c), vec(c), full(e128)]
    if vres is not None:
        v_first, v0, vdn, vup = vres
        args += [v_first, v0, vdn, vup]
        specs += [row(c), vec(c), full(vdn), full(vup)]
    return pl.pallas_call(
        functools.partial(_rwkv_prep_kernel, has_vres=vres is not None),
        out_shape=tuple(jax.ShapeDtypeStruct((t, c), F32) for _ in range(8)),
        grid=(t // tm,),
        in_specs=specs,
        out_specs=tuple(row(c) for _ in range(8)),
        compiler_params=_cparams(1), name="rwkv_prep",
    )(*args)


def _rwkv_rec_kernel(r_ref, w_ref, k_ref, v_ref, kk_ref, b_ref, s0_ref, y_ref, st_ref, state):
    nb, tc = r_ref.shape[:2]
    c = pl.program_id(1)

    @pl.when(c == 0)
    def _():
        state[...] = s0_ref[...]

    lane = lax.broadcasted_iota(I32, (HEAD_DIM, LANES), 1)
    row = lax.broadcasted_iota(I32, (HEAD_DIM, LANES), 0)
    low = lane < HEAD_DIM
    eye = jnp.where(low, lane, lane - HEAD_DIM) == row

    def seg(x):
        e = jnp.sum(jnp.where(low, x, 0.0), axis=1, keepdims=True)
        o = jnp.sum(jnp.where(low, 0.0, x), axis=1, keepdims=True)
        return jnp.where(low, e, o)

    def step(t, carry):
        for bi in range(nb):
            r_t, w_t, k_t, v_t, kk_t, b_t = (ref[bi, t] for ref in
                                             (r_ref, w_ref, k_ref, v_ref, kk_ref, b_ref))
            y_rows = []
            for p in range(N_PAIRS):
                s_old = state[bi, p]
                sa = seg(s_old * (-kk_t[p:p + 1, :]))
                v_col = seg(jnp.where(eye, v_t[p:p + 1, :], 0.0))
                s_new = s_old * w_t[p:p + 1, :] + sa * b_t[p:p + 1, :] + v_col * k_t[p:p + 1, :]
                state[bi, p] = s_new
                y_b = seg(s_new * r_t[p:p + 1, :])
                y_rows.append(jnp.sum(jnp.where(eye, y_b, 0.0), axis=0, keepdims=True))
            y_ref[bi, t] = jnp.concatenate(y_rows, axis=0)
        return carry

    lax.fori_loop(0, tc, step, 0)

    @pl.when(c == pl.num_programs(1) - 1)
    def _():
        st_ref[...] = state[...]


def _rwkv_rec(seqs, s0, nb, tc):
    b, t = seqs[0].shape[:2]
    blk = pl.BlockSpec((nb, tc, N_PAIRS, LANES), lambda g, i: (g, i, 0, 0))
    st = pl.BlockSpec((nb, N_PAIRS, HEAD_DIM, LANES), lambda g, i: (g, 0, 0, 0))
    return pl.pallas_call(
        _rwkv_rec_kernel,
        out_shape=(jax.ShapeDtypeStruct((b, t, N_PAIRS, LANES), F32),
                   jax.ShapeDtypeStruct((b, N_PAIRS, HEAD_DIM, LANES), F32)),
        grid=(b // nb, t // tc),
        in_specs=[blk] * 6 + [st],
        out_specs=(blk, st),
        scratch_shapes=[pltpu.VMEM((nb, N_PAIRS, HEAD_DIM, LANES), F32)],
        compiler_params=_cparams(2), name="rwkv_rec",
    )(*seqs, s0)


def _sp(x):
    return _split2(x)


def _mm3(a, b, mode="nn"):
    if mode == "nt":
        d = _dot_nt
    elif mode == "tn":
        d = lambda x, y: lax.dot_general(x, y, (((0,), (0,)), ((), ())), preferred_element_type=F32)
    else:
        d = _dot
    return d(a[0], b[0]) + (d(a[0], b[1]) + d(a[1], b[0]))


def _rwkv_chunk_kernel(r_ref, lw_ref, k_ref, v_ref, kk_ref, b_ref, y_ref, st_ref, hbd, *, pairs):
    c = pl.program_id(2)
    n_c = r_ref.shape[0]
    n2 = 2 * n_c
    log_c = _log2(n_c)

    @pl.when(c == 0)
    def _():
        hbd[...] = jnp.zeros(hbd.shape, F32)

    lane = lax.broadcasted_iota(I32, (n_c, LANES), 1)
    low = lane < HEAD_DIM
    i = lax.broadcasted_iota(I32, (n2, n2), 0)
    j = lax.broadcasted_iota(I32, (n2, n2), 1)
    same = (i >> log_c) == (j >> log_c)
    ti = i & (n_c - 1)
    tj = j & (n_c - 1)
    m_stril = jnp.where(same & (tj < ti), 1.0, 0.0)
    m_tril = jnp.where(same & (tj <= ti), 1.0, 0.0)
    eye = jnp.where(i == j, 1.0, 0.0)
    m_blk = {s: jnp.where((i >> s) == (j >> s), 1.0, 0.0) for s in range(3, log_c + 1)}
    ci = lax.broadcasted_iota(I32, (n_c, n_c), 0)
    cj = lax.broadcasted_iota(I32, (n_c, n_c), 1)
    tri_c = jnp.where(cj <= ci, 1.0, 0.0).astype(BF16)

    def stack(x):
        return jnp.concatenate([jnp.where(low, x, 0.0), jnp.where(low, 0.0, x)], axis=0)

    def each(fn, *cols):
        return [fn(*args) for args in zip(*cols)]

    sls = [slice(q * LANES, (q + 1) * LANES) for q in range(pairs)]
    lw = [lw_ref[:, sl] for sl in sls]
    cum = each(lambda x: _dot_exact_lhs(tri_c, x), lw)
    cum_c = each(lambda x: x[n_c - 1:n_c, :], cum)
    e_neg = each(lambda x: jnp.exp(-x), cum)
    e_tail = each(lambda x, xc: jnp.exp(xc - x), cum, cum_c)
    kk = [kk_ref[:, sl] for sl in sls]
    b = [b_ref[:, sl] for sl in sls]
    k = [k_ref[:, sl] for sl in sls]
    a_s = each(lambda x, cu, l: _sp(stack(-x * jnp.exp(cu - l))), kk, cum, lw)
    b_s = each(lambda x, e: _sp(stack(x * e)), b, e_neg)
    k_s = each(lambda x, e: _sp(stack(x * e)), k, e_neg)
    r_st = [stack(r_ref[:, sl] * jnp.exp(cu)) for sl, cu in zip(sls, cum)]
    r_s = each(_sp, r_st)
    bh_s = each(lambda x, e: _sp(stack(x * e)), b, e_tail)
    kh_s = each(lambda x, e: _sp(stack(x * e)), k, e_tail)
    v_s = [_sp(stack(v_ref[:, sl])) for sl in sls]

    l_ab = each(lambda x, y: m_stril * _mm3(x, y, "nt"), a_s, b_s)
    l_ak = each(lambda x, y: _sp(m_stril * _mm3(x, y, "nt")), a_s, k_s)
    m_rb = each(lambda x, y: _sp(m_tril * _mm3(x, y, "nt")), r_s, b_s)
    m_rk = each(lambda x, y: _sp(m_tril * _mm3(x, y, "nt")), r_s, k_s)

    d1 = each(lambda x: x * m_blk[3], l_ab)
    d1s = each(_sp, d1)
    d2 = each(lambda x: _mm3(x, x), d1s)
    d2s = each(_sp, d2)
    d4 = each(lambda x: _mm3(x, x), d2s)
    i12 = each(lambda x, y: _sp(_mm3(_sp(eye + x), _sp(eye + y))), d1, d2)
    inv = each(lambda x, y: _mm3(x, _sp(eye + y)), i12, d4)
    for s in range(4, log_c + 1):
        l_m = each(lambda x: _sp(x * (m_blk[s] - m_blk[s - 1])), l_ab)
        inv_s = each(_sp, inv)
        t_m = each(lambda x, y: _sp(_mm3(x, y)), inv_s, l_m)
        inv = each(lambda x, y, z: x + _mm3(y, z), inv, t_m, inv_s)
    inv_s = each(_sp, inv)
    w1 = each(lambda x, y: _sp(_mm3(x, y)), inv_s, a_s)
    lv = each(lambda x, y: _sp(_mm3(x, y)), l_ak, v_s)
    u0 = each(lambda x, y: _sp(_mm3(x, y)), inv_s, lv)
    y1 = each(lambda x, y, z: x + _mm3(y, z), r_st, m_rb, w1)
    y0 = each(lambda x, y, z, u: _mm3(x, y) + _mm3(z, u), m_rb, u0, m_rk, v_s)
    g = each(lambda xc, x, y: _sp(eye * jnp.exp(xc) + _mm3(x, y, "tn")), cum_c, bh_s, w1)
    h_add = each(lambda x, y, z, u: _mm3(x, y, "tn") + _mm3(z, u, "tn"), bh_s, u0, kh_s, v_s)
    y1c = each(lambda x: _sp(x[:n_c] + x[n_c:]), y1)
    y0c = each(lambda x: x[:n_c] + x[n_c:], y0)

    h_s = [_sp(hbd[q]) for q in range(pairs)]
    for q in range(pairs):
        y_ref[:, sls[q]] = _mm3(y1c[q], h_s[q]) + y0c[q]
    for q in range(pairs):
        hbd[q] = _mm3(g[q], h_s[q]) + h_add[q]

    @pl.when(c == pl.num_programs(2) - 1)
    def _():
        st_ref[0] = hbd[...]


def _rwkv_chunked(seqs, n_seq, seq_len, chunk, pairs):
    n_chunks = seq_len // chunk
    n_grp = N_PAIRS // pairs
    blk = pl.BlockSpec((chunk, pairs * LANES), lambda b, p, c: (b * n_chunks + c, p))
    return pl.pallas_call(
        functools.partial(_rwkv_chunk_kernel, pairs=pairs),
        out_shape=(jax.ShapeDtypeStruct((n_seq * seq_len, C_HEADS), F32),
                   jax.ShapeDtypeStruct((n_seq * n_grp, pairs, LANES, LANES), F32)),
        grid=(n_seq, n_grp, n_chunks),
        in_specs=[blk] * 6,
        out_specs=(blk, pl.BlockSpec((1, pairs, LANES, LANES), lambda b, p, c: (b * n_grp + p, 0, 0, 0))),
        scratch_shapes=[pltpu.VMEM((pairs, LANES, LANES), F32)],
        compiler_params=_cparams(3), name="rwkv_chunked",
    )(*seqs)


def _unpack_hbd(h, n_seq):
    h = h.reshape(n_seq, N_PAIRS, 2, HEAD_DIM, 2, HEAD_DIM)
    diag = jnp.stack([h[:, :, 0, :, 0, :], h[:, :, 1, :, 1, :]], axis=2)
    return jnp.transpose(diag, (0, 1, 2, 4, 3)).reshape(n_seq, N_HEADS, HEAD_DIM, HEAD_DIM)


def _rwkv_post_kernel(y_ref, r_ref, k_ref, v_ref, g_ref, lnw_ref, lnb_ref, rk_ref, e_ref, o_ref):
    y = y_ref[...]
    inv = 1.0 / HEAD_DIM
    mean = _head_segsum(y, e_ref) * inv
    yc = y - mean
    var = _head_segsum(yc * yc, e_ref) * inv
    yn = yc * lax.rsqrt(var + LN_X_EPS) * lnw_ref[...] + lnb_ref[...]
    bonus = _head_segsum(r_ref[...] * k_ref[...] * rk_ref[...], e_ref) * v_ref[...]
    o_ref[...] = ((yn + bonus) * g_ref[...]).astype(o_ref.dtype)


def _rwkv_post(y, r, k, v, g, ln_w, ln_b, r_k, e128):
    t, c = y.shape
    row = pl.BlockSpec((ROW_TILE, c), lambda i: (i, 0))
    vec = pl.BlockSpec((1, c), lambda i: (0, 0))
    return pl.pallas_call(
        _rwkv_post_kernel,
        out_shape=jax.ShapeDtypeStruct((t, c), BF16),
        grid=(t // ROW_TILE,),
        in_specs=[row] * 5 + [vec] * 3 + [pl.BlockSpec(e128.shape, lambda i: (0, 0))],
        out_specs=row,
        compiler_params=_cparams(1), name="rwkv_post",
    )(y, r, k, v, g, ln_w, ln_b, r_k, e128)


def _pack_state(s):
    b = s.shape[0]
    s = s.reshape(b, N_PAIRS, 2, HEAD_DIM, HEAD_DIM)
    return jnp.transpose(s, (0, 1, 3, 2, 4)).reshape(b, N_PAIRS, HEAD_DIM, LANES)


def _unpack_state(s):
    b = s.shape[0]
    s = s.reshape(b, N_PAIRS, HEAD_DIM, 2, HEAD_DIM)
    return jnp.transpose(s, (0, 1, 3, 2, 4)).reshape(b, N_HEADS, HEAD_DIM, HEAD_DIM)


def _expert_kernel(src_ref, te_ref, nu_ref, h_ref, w1_ref, b1_ref, w2_ref, b2_ref, sel_ref, o_ref,
                   xbuf, xb, sems):
    t = pl.program_id(0)
    j = pl.program_id(1)
    nf = pl.num_programs(1)
    tm = o_ref.shape[0]
    per_step = tm // nf

    def row_copy(tile, r, priority=None):
        slot = tile % 2
        cp = pltpu.make_async_copy(h_ref.at[pl.ds(src_ref[tile * tm + r], 1), :],
                                   xbuf.at[slot, pl.ds(r, 1), :], sems.at[slot])
        if priority is None:
            return cp
        cp.start(priority=priority)

    @pl.when(t < nu_ref[0])
    def _():
        @pl.when((t == 0) & (j == 0))
        def _():
            def first(r, carry):
                row_copy(0, r, 0)
                return carry
            lax.fori_loop(0, tm, first, 0)

        def wait_tile(tile):
            slot = tile % 2
            pltpu.make_async_copy(h_ref.at[pl.ds(0, tm), :], xbuf.at[slot], sems.at[slot]).wait()

        @pl.when(j == 0)
        def _():
            wait_tile(t)
            xb[...] = xbuf[t % 2].astype(BF16)

        nxt_tile = jnp.minimum(t + 1, nu_ref[0] - 1)
        for r in range(per_step):
            pltpu.make_async_copy(h_ref.at[pl.ds(src_ref[nxt_tile * tm + j * per_step + r], 1), :],
                                  xbuf.at[(t + 1) % 2, pl.ds(j * per_step + r, 1), :],
                                  sems.at[(t + 1) % 2]).start(priority=r % 2)

        w1b = w1_ref[0].astype(BF16)
        w2b = w2_ref[0].astype(BF16)
        sub = tm // EXPERT_SUBTILES
        rows = [slice(s * sub, (s + 1) * sub) for s in range(EXPERT_SUBTILES)]
        hh = [_dot(xb[rs, :], w1b) + b1_ref[0] for rs in rows]
        width = hh[0].shape[1]
        nxt = [pltpu.roll(h, width - 1, 1) for h in hh]
        glu = [jnp.minimum(h, SWIGLU_LIMIT) for h in hh]
        act = [g * _sigmoid(SWIGLU_ALPHA * g) * (jnp.clip(n, -SWIGLU_LIMIT, SWIGLU_LIMIT) + 1.0)
               for g, n in zip(glu, nxt)]
        act = [_dot(a.astype(BF16), sel_ref[...]).astype(BF16) for a in act]
        part = [_dot(a, w2b) for a in act]

        @pl.when(j == 0)
        def _():
            for rs, p in zip(rows, part):
                o_ref[rs, :] = p + b2_ref[0]

        @pl.when(j > 0)
        def _():
            for rs, p in zip(rows, part):
                o_ref[rs, :] = o_ref[rs, :] + p

        @pl.when((t == nu_ref[0] - 1) & (j == nf - 1))
        def _():
            wait_tile(t + 1)

    @pl.when((t >= nu_ref[0]) & (j == 0))
    def _():
        o_ref[...] = jnp.zeros(o_ref.shape, F32)


def _expert_mlp(h, src, tile_expert, n_used, w1, b1, w2, b2):
    n_rows = src.shape[0]
    tm, bf = EXPERT_TILE, EXPERT_FBLOCK
    n_tiles = n_rows // tm
    nf = D_EXPERT // bf
    sel = (jnp.arange(2 * bf, dtype=I32)[:, None] == 2 * jnp.arange(bf, dtype=I32)[None, :]).astype(BF16)
    return pl.pallas_call(
        _expert_kernel,
        out_shape=jax.ShapeDtypeStruct((n_rows, D_MODEL), F32),
        grid_spec=pltpu.PrefetchScalarGridSpec(
            num_scalar_prefetch=3, grid=(n_tiles, nf),
            in_specs=[pl.BlockSpec(memory_space=pl.ANY),
                      pl.BlockSpec((1, D_MODEL, 2 * bf), lambda t, j, src, te, nu: (te[t], 0, j)),
                      pl.BlockSpec((1, 1, 2 * bf), lambda t, j, src, te, nu: (te[t], 0, j)),
                      pl.BlockSpec((1, bf, D_MODEL), lambda t, j, src, te, nu: (te[t], j, 0)),
                      pl.BlockSpec((1, 1, D_MODEL), lambda t, j, src, te, nu: (te[t], 0, 0)),
                      pl.BlockSpec((2 * bf, bf), lambda t, j, src, te, nu: (0, 0))],
            out_specs=pl.BlockSpec((tm, D_MODEL), lambda t, j, src, te, nu: (t, 0)),
            scratch_shapes=[pltpu.VMEM((2, tm, D_MODEL), F32), pltpu.VMEM((tm, D_MODEL), BF16),
                            pltpu.SemaphoreType.DMA((2,))]),
        compiler_params=_cparams(2), name="expert_mlp",
    )(src, tile_expert, n_used, h, w1.reshape(-1, D_MODEL, 2 * D_EXPERT), b1.reshape(-1, 1, 2 * D_EXPERT),
      w2.reshape(-1, D_EXPERT, D_MODEL), b2.reshape(-1, 1, D_MODEL), sel)


def _combine_kernel(slot_ref, ys_ref, x_ref, route_ref, gt_ref, o_ref, ybuf, sem, *, grp):
    rows = x_ref.shape[0]
    base = pl.program_id(0) * rows

    def copy(r, j):
        return pltpu.make_async_copy(ys_ref.at[pl.ds(slot_ref[(base + r) * TOP_K + j], 1), :],
                                     ybuf.at[j, pl.ds(r, 1), :], sem)

    def start(r, carry):
        for j in range(TOP_K):
            copy(r, j).start(priority=j % 2)
        return carry

    def wait(r, carry):
        for j in range(TOP_K):
            copy(r, j).wait()
        return carry

    lax.fori_loop(0, rows, start, 0, unroll=4)
    lax.fori_loop(0, rows, wait, 0, unroll=4)
    route = route_ref[...]
    moe = jnp.zeros(x_ref.shape, F32)
    for j in range(TOP_K):
        moe = moe + route[:, TOP_K + j:TOP_K + j + 1] * ybuf[j]
    oh = _seq_onehot(base, rows, grp)
    o_ref[...] = x_ref[...] + _dot_exact_lhs(oh, gt_ref[...]) * moe


def _combine(x, ys, slots, route, mod, k_gt, grp):
    t = x.shape[0]
    tm = 128
    return pl.pallas_call(
        functools.partial(_combine_kernel, grp=grp),
        out_shape=jax.ShapeDtypeStruct((t, D_MODEL), F32),
        grid_spec=pltpu.PrefetchScalarGridSpec(
            num_scalar_prefetch=1, grid=(t // tm,),
            in_specs=[pl.BlockSpec(memory_space=pl.ANY),
                      pl.BlockSpec((tm, D_MODEL), lambda i, s: (i, 0)),
                      pl.BlockSpec((tm, LANES), lambda i, s: (i, 0)),
                      pl.BlockSpec((SEQ_TABLE_ROWS, D_MODEL), lambda i, s, k=k_gt: (0, k))],
            out_specs=pl.BlockSpec((tm, D_MODEL), lambda i, s: (i, 0)),
            scratch_shapes=[pltpu.VMEM((TOP_K, tm, D_MODEL), F32), pltpu.SemaphoreType.DMA]),
        compiler_params=_cparams(1), name="moe_combine",
    )(slots, ys, x, route, mod)


def _moe_layer(x, g, mod, wr_pad, br_pad, w1, b1, w2, b2, layer, grp):
    t = x.shape[0]
    h2, route, counts = _rms_router(x, g, mod, 4, 3, wr_pad, br_pad, grp)
    tm = EXPERT_TILE
    n_tiles = (t * TOP_K) // tm + N_EXPERTS
    cnt = counts[0, :N_EXPERTS].astype(I32)
    tiles_per = (cnt + tm - 1) // tm
    tile_end = jnp.cumsum(tiles_per)
    group_start = (tile_end - tiles_per) * tm
    idx = route[:, :TOP_K].astype(I32)
    pos = route[:, 2 * TOP_K:3 * TOP_K].astype(I32)
    slots = (group_start[idx] + pos).reshape(-1)
    token_of_pair = jnp.repeat(jnp.arange(t, dtype=I32), TOP_K)
    src = jnp.zeros((n_tiles * tm,), I32).at[slots].set(token_of_pair)
    n_used = tile_end[-1:].astype(I32)
    tile_expert = jnp.searchsorted(tile_end, jnp.arange(n_tiles, dtype=I32), side="right").astype(I32)
    tile_expert = jnp.minimum(tile_expert, N_EXPERTS - 1)
    tile_expert = jnp.where(jnp.arange(n_tiles) < n_used[0], tile_expert,
                            tile_expert[jnp.maximum(n_used[0] - 1, 0)])
    ys = _expert_mlp(h2, src, tile_expert + layer * N_EXPERTS, n_used, w1, b1, w2, b2)
    return _combine(x, ys, slots, route, mod, 5, grp)


def _log2(n):
    s = n.bit_length() - 1
    assert (1 << s) == n, "sequence lengths must be powers of two"
    return s


def kernel(x_prompt, x_sample, cache_k, cache_v, page_table, state_shift, state_wkv, c_prompt, c_sample,
           w_ada, b_ada, norm_mix, norm_ffn, w_in, w_out, rw_mu, rw_w0, rw_w_up, rw_a0, rw_a_up,
           rw_g_up, rw_k_k, rw_k_a, rw_r_k, rw_ln_w, rw_ln_b, rw_v0, rw_v_down, rw_v_up,
           w_router, b_router, w_mlp1, b_mlp1, w_mlp2, b_mlp2, norm_final):
    bp, tp, d = x_prompt.shape
    bs, ts, _ = x_sample.shape
    depth = w_ada.shape[0]
    n_pool = cache_k.shape[1]
    np_rows, ns_rows = bp * tp, bs * ts
    t_all = np_rows + ns_rows
    assert d == D_MODEL and t_all % ROW_TILE == 0 and np_rows % ROW_TILE == 0
    assert bp + bs <= SEQ_TABLE_ROWS and tp % MOBA_BLOCK == 0 and ts <= SUBLANES
    grp = (np_rows, _log2(tp), _log2(ts), bp)
    c = C_HEADS

    x = jnp.concatenate([x_prompt.reshape(np_rows, d), x_sample.reshape(ns_rows, d)], axis=0)
    c_pad = jnp.zeros((SEQ_TABLE_ROWS, d), F32).at[:bp + bs].set(jnp.concatenate([c_prompt, c_sample], axis=0))
    mod = _ada(c_pad, w_ada, b_ada)

    cache_kt = jnp.transpose(cache_k, (0, 1, 3, 4, 2)).reshape(depth * n_pool, c, PAGE_SIZE)
    cache_vt = jnp.transpose(cache_v, (0, 1, 3, 4, 2)).reshape(depth * n_pool, c, PAGE_SIZE)
    e128 = (jnp.arange(LANES)[:, None] // HEAD_DIM == jnp.arange(LANES)[None, :] // HEAD_DIM).astype(BF16)

    def lora_pad(w, start):
        return jnp.zeros((C_LORA_PAD, c), F32).at[start:start + w.shape[0]].set(w)

    k_out, v_out, shift_p, shift_s, wkv_p, wkv_s = [], [], [], [], [], []
    v_first = None
    for l in range(depth):
        h = _rms_mod(x, norm_mix[l], mod[l], 1, 0, grp)
        w_qkv = w_in[l][:, :3 * c].astype(BF16)
        w_pr = jnp.zeros((d, C_SHIFT_PAD), BF16).at[:, :C_SHIFT].set(w_in[l][:, 3 * c:].astype(BF16))
        qkv = _matmul(h, w_qkv, 768, 1024)
        pr = _matmul(h, w_pr, 768, 512)

        att_p = _prompt_attention(qkv, bp, tp)
        att_s = _sample_attention(qkv, np_rows, bs, ts, cache_kt, cache_vt, page_table, l * n_pool)
        att = jnp.concatenate([att_p, att_s.astype(BF16)], axis=0)

        pr_p = pr[:np_rows].reshape(bp, tp, C_SHIFT_PAD)
        pr_s = pr[np_rows:].reshape(bs, ts, C_SHIFT_PAD)
        shift0_s = jnp.zeros((bs, 1, C_SHIFT_PAD), F32).at[:, 0, :C_SHIFT].set(state_shift[l])
        prev = jnp.concatenate([
            jnp.concatenate([jnp.zeros((bp, 1, C_SHIFT_PAD), F32), pr_p[:, :-1]], axis=1).reshape(np_rows, -1),
            jnp.concatenate([shift0_s, pr_s[:, :-1]], axis=1).reshape(ns_rows, -1)], axis=0)
        mu = jnp.zeros((1, C_SHIFT_PAD), F32).at[0, :C_SHIFT].set(rw_mu[l])
        vecs = (mu, rw_w0[l].reshape(1, c), rw_a0[l].reshape(1, c), rw_k_k[l].reshape(1, c),
                rw_k_a[l].reshape(1, c))
        mats = (lora_pad(rw_w_up[l], 0), lora_pad(rw_a_up[l], D_DECAY_LORA),
                lora_pad(rw_g_up[l], D_DECAY_LORA + D_AAA_LORA))
        vres = None
        if l > 0:
            vdn = jnp.zeros((c, LANES), F32).at[:, :D_MV_LORA].set(rw_v_down[l - 1])
            vup = jnp.zeros((LANES, c), F32).at[:D_MV_LORA].set(rw_v_up[l - 1])
            vres = (v_first, rw_v0[l - 1].reshape(1, c), vdn, vup)
        r, w, lw, k, v, kk, b, g = _rwkv_prep(pr, prev, vecs, mats, e128, vres)
        if l == 0:
            v_first = v
        y_p, h_p = _rwkv_chunked((r, lw, k, v, kk, b), bp, tp, RWKV_CHUNK, RWKV_PAIRS_PER_STEP)
        seqs_s = [a[np_rows:].reshape(bs, ts, N_PAIRS, LANES) for a in (r, w, k, v, kk, b)]
        y_s, st_s = _rwkv_rec(seqs_s, _pack_state(state_wkv[l]), 4, ts)
        y = jnp.concatenate([y_p, y_s.reshape(ns_rows, c)], axis=0)
        rw = _rwkv_post(y, r, k, v, g, rw_ln_w[l].reshape(1, c), rw_ln_b[l].reshape(1, c),
                        rw_r_k[l].reshape(1, c), e128)

        x = _mix_out(att, rw, w_out[l].astype(BF16), x, mod[l], 2, grp)

        wr_pad = jnp.zeros((d, LANES), F32).at[:, :N_EXPERTS].set(w_router[l])
        br_pad = jnp.full((1, LANES), NEG_BIG, F32).at[0, :N_EXPERTS].set(b_router[l])
        x = _moe_layer(x, norm_ffn[l], mod[l], wr_pad, br_pad, w_mlp1, b_mlp1, w_mlp2, b_mlp2, l, grp)

        k_out.append(qkv[:, c:2 * c])
        v_out.append(qkv[:, 2 * c:3 * c])
        shift_p.append(pr_p[:, -1, :C_SHIFT])
        shift_s.append(pr_s[:, -1, :C_SHIFT])
        wkv_p.append(_unpack_hbd(h_p.reshape(bp * N_PAIRS, LANES, LANES), bp))
        wkv_s.append(_unpack_state(st_s))

    y = _final_norm(x, norm_final)
    k_all, v_all = jnp.stack(k_out), jnp.stack(v_out)
    pages = tp // PAGE_SIZE
    return (y[:np_rows].reshape(bp, tp, d), y[np_rows:].reshape(bs, ts, d),
            k_all[:, :np_rows].reshape(depth, bp, pages, PAGE_SIZE, N_HEADS, HEAD_DIM),
            v_all[:, :np_rows].reshape(depth, bp, pages, PAGE_SIZE, N_HEADS, HEAD_DIM),
            jnp.stack(shift_p), jnp.stack(wkv_p),
            k_all[:, np_rows:].reshape(depth, bs, ts, N_HEADS, HEAD_DIM),
            v_all[:, np_rows:].reshape(depth, bs, ts, N_HEADS, HEAD_DIM),
            jnp.stack(shift_s), jnp.stack(wkv_s))
```

```python
import functools

import jax
import jax.numpy as jnp
from jax import lax
from jax.experimental import pallas as pl
from jax.experimental.pallas import tpu as pltpu

F32, BF16, I32 = jnp.float32, jnp.bfloat16, jnp.int32

LANES = 128
SUBLANES = 8
VMEM_LIMIT = 50 * 1024 * 1024

D_MODEL = 2048
HEAD_DIM = 64
N_HEADS = 16
C_HEADS = N_HEADS * HEAD_DIM
N_PAIRS = C_HEADS // LANES
MOBA_BLOCK = 256
MOBA_TOPK = 3
PAGE_SIZE = 128
D_DECAY_LORA, D_AAA_LORA, D_GATE_LORA, D_MV_LORA = 64, 64, 160, 32
C_SHIFT = 3 * C_HEADS + D_DECAY_LORA + D_AAA_LORA + D_GATE_LORA
C_SHIFT_PAD = 3584
C_LORA_PAD = C_SHIFT_PAD - 3 * C_HEADS
LN_X_EPS = 64e-5
RMS_EPS = 1e-5
N_EXPERTS = 32
TOP_K = 4
D_EXPERT = D_MODEL
SWIGLU_LIMIT = 7.0
SWIGLU_ALPHA = 1.702
NEG_BIG = -3.0e38

ROW_TILE = 256
SEQ_TABLE_ROWS = 128
EXPERT_TILE = 576
EXPERT_FBLOCK = 256
EXPERT_SUBTILES = 3
RWKV_CHUNK = 64
RWKV_PAIRS_PER_STEP = 8
SAMPLE_PAGES_PER_STEP = 8
PROMPT_PAIRS_PER_STEP = 4


def _cparams(n_axes):
    return pltpu.CompilerParams(dimension_semantics=("arbitrary",) * n_axes,
                                vmem_limit_bytes=VMEM_LIMIT)


def _dot(a, b):
    return jnp.dot(a, b, preferred_element_type=F32)


def _dot_nt(a, b):
    return lax.dot_general(a, b, (((1,), (1,)), ((), ())), preferred_element_type=F32)


def _split2(x):
    hi = x.astype(BF16)
    lo = (x - hi.astype(F32)).astype(BF16)
    return hi, lo


def _split3(x):
    hi = x.astype(BF16)
    r = x - hi.astype(F32)
    mid = r.astype(BF16)
    lo = (r - mid.astype(F32)).astype(BF16)
    return hi, mid, lo


def _dot_ref(a, b, nt=False):
    d = _dot_nt if nt else _dot
    return d(a.astype(BF16), b.astype(BF16))


def _dot_exact_lhs(a_bf16, b):
    bh, bm, bl = _split3(b)
    return _dot(a_bf16, bh) + (_dot(a_bf16, bm) + _dot(a_bf16, bl))


def _seq_onehot(row0, rows, grp):
    n_prompt_rows, tp_shift, ts_shift, n_prompt = grp
    r = row0 + lax.broadcasted_iota(I32, (rows, SEQ_TABLE_ROWS), 0)
    lane = lax.broadcasted_iota(I32, (rows, SEQ_TABLE_ROWS), 1)
    sid = jnp.where(r < n_prompt_rows, r >> tp_shift, n_prompt + ((r - n_prompt_rows) >> ts_shift))
    return jnp.where(lane == sid, 1.0, 0.0).astype(BF16)


def _head_segsum(x, e_ref):
    e = e_ref[...]
    outs = []
    for c in range(x.shape[1] // LANES):
        hi, lo = _split2(x[:, c * LANES:(c + 1) * LANES])
        outs.append(_dot(hi, e) + _dot(lo, e))
    return jnp.concatenate(outs, axis=1)


def _sigmoid(x):
    return 1.0 / (1.0 + jnp.exp(-x))


def _ada_kernel(c_ref, w_ref, b_ref, o_ref):
    o_ref[0] = _dot_ref(c_ref[...], w_ref[0]) + b_ref[0]


def _ada(c_pad, w_ada, b_ada):
    depth, d, n = w_ada.shape
    tn = 512
    return pl.pallas_call(
        _ada_kernel,
        out_shape=jax.ShapeDtypeStruct((depth, SEQ_TABLE_ROWS, n), F32),
        grid=(depth, n // tn),
        in_specs=[pl.BlockSpec((SEQ_TABLE_ROWS, d), lambda l, j: (0, 0)),
                  pl.BlockSpec((1, d, tn), lambda l, j: (l, 0, j)),
                  pl.BlockSpec((1, 1, tn), lambda l, j: (l, 0, j))],
        out_specs=pl.BlockSpec((1, SEQ_TABLE_ROWS, tn), lambda l, j: (l, 0, j)),
        compiler_params=_cparams(2), name="ada_mod",
    )(c_pad, w_ada, b_ada.reshape(depth, 1, n))


def _rms_modulated(x_ref, g_ref, sc_ref, sh_ref, grp):
    rows = x_ref.shape[0]
    x = x_ref[...]
    y = x * lax.rsqrt(jnp.mean(x * x, axis=-1, keepdims=True) + RMS_EPS) * g_ref[...]
    oh = _seq_onehot(pl.program_id(0) * rows, rows, grp)
    sc = _dot_exact_lhs(oh, sc_ref[...])
    sh = _dot_exact_lhs(oh, sh_ref[...])
    return y * (1.0 + sc) + sh


def _rms_mod_kernel(x_ref, g_ref, sc_ref, sh_ref, o_ref, *, grp):
    o_ref[...] = _rms_modulated(x_ref, g_ref, sc_ref, sh_ref, grp).astype(o_ref.dtype)


def _mod_spec(k):
    return pl.BlockSpec((SEQ_TABLE_ROWS, D_MODEL), lambda i, k=k: (0, k))


def _rms_mod(x, g, mod, k_sc, k_sh, grp):
    t = x.shape[0]
    return pl.pallas_call(
        functools.partial(_rms_mod_kernel, grp=grp),
        out_shape=jax.ShapeDtypeStruct((t, D_MODEL), BF16),
        grid=(t // ROW_TILE,),
        in_specs=[pl.BlockSpec((ROW_TILE, D_MODEL), lambda i: (i, 0)),
                  pl.BlockSpec((1, D_MODEL), lambda i: (0, 0)),
                  _mod_spec(k_sc), _mod_spec(k_sh)],
        out_specs=pl.BlockSpec((ROW_TILE, D_MODEL), lambda i: (i, 0)),
        compiler_params=_cparams(1), name="rms_mod",
    )(x, g.reshape(1, D_MODEL), mod, mod)


def _rms_router_kernel(x_ref, g_ref, sc_ref, sh_ref, wr_ref, br_ref, h_ref, route_ref, cnt_ref,
                       carry, *, grp):
    i = pl.program_id(0)
    rows = x_ref.shape[0]
    h = _rms_modulated(x_ref, g_ref, sc_ref, sh_ref, grp)
    h_ref[...] = h
    logits = _dot_ref(h, wr_ref[...]) + br_ref[...]
    lane = lax.broadcasted_iota(I32, (rows, LANES), 1)
    vals, idxs = [], []
    multi = jnp.zeros((rows, LANES), F32)
    for _ in range(TOP_K):
        m = jnp.max(logits, axis=1, keepdims=True)
        sel = jnp.min(jnp.where(logits == m, lane, LANES), axis=1, keepdims=True)
        hit = lane == sel
        vals.append(m)
        idxs.append(sel)
        multi = jnp.where(hit, 1.0, multi)
        logits = jnp.where(hit, NEG_BIG, logits)
    es = [jnp.exp(v - vals[0]) for v in vals]
    denom = es[0] + es[1] + es[2] + es[3]

    @pl.when(i == 0)
    def _():
        carry[...] = jnp.zeros_like(carry)

    r_i = lax.broadcasted_iota(I32, (rows, rows), 0)
    c_i = lax.broadcasted_iota(I32, (rows, rows), 1)
    tri = jnp.where(c_i < r_i, 1.0, 0.0).astype(BF16)
    cum = _dot(tri, multi.astype(BF16)) + carry[0:1, :]
    route = jnp.zeros((rows, LANES), F32)
    for j in range(TOP_K):
        pos = jnp.sum(jnp.where(lane == idxs[j], cum, 0.0), axis=1, keepdims=True)
        route = jnp.where(lane == j, idxs[j].astype(F32), route)
        route = jnp.where(lane == TOP_K + j, es[j] / denom, route)
        route = jnp.where(lane == 2 * TOP_K + j, pos, route)
    route_ref[...] = route
    new_cnt = carry[0:1, :] + jnp.sum(multi, axis=0, keepdims=True)
    carry[...] = jnp.broadcast_to(new_cnt, carry.shape)
    cnt_ref[...] = jnp.broadcast_to(new_cnt, cnt_ref.shape)


def _rms_router(x, g, mod, k_sc, k_sh, wr_pad, br_pad, grp):
    t = x.shape[0]
    return pl.pallas_call(
        functools.partial(_rms_router_kernel, grp=grp),
        out_shape=(jax.ShapeDtypeStruct((t, D_MODEL), F32),
                   jax.ShapeDtypeStruct((t, LANES), F32),
                   jax.ShapeDtypeStruct((SUBLANES, LANES), F32)),
        grid=(t // ROW_TILE,),
        in_specs=[pl.BlockSpec((ROW_TILE, D_MODEL), lambda i: (i, 0)),
                  pl.BlockSpec((1, D_MODEL), lambda i: (0, 0)),
                  _mod_spec(k_sc), _mod_spec(k_sh),
                  pl.BlockSpec((D_MODEL, LANES), lambda i: (0, 0)),
                  pl.BlockSpec((1, LANES), lambda i: (0, 0))],
        out_specs=(pl.BlockSpec((ROW_TILE, D_MODEL), lambda i: (i, 0)),
                   pl.BlockSpec((ROW_TILE, LANES), lambda i: (i, 0)),
                   pl.BlockSpec((SUBLANES, LANES), lambda i: (0, 0))),
        scratch_shapes=[pltpu.VMEM((SUBLANES, LANES), F32)],
        compiler_params=_cparams(1), name="rms_router",
    )(x, g.reshape(1, D_MODEL), mod, mod, wr_pad, br_pad)


def _final_norm_kernel(x_ref, g_ref, o_ref):
    x = x_ref[...]
    o_ref[...] = x * lax.rsqrt(jnp.mean(x * x, axis=-1, keepdims=True) + RMS_EPS) * g_ref[...]


def _final_norm(x, g):
    t = x.shape[0]
    return pl.pallas_call(
        _final_norm_kernel,
        out_shape=jax.ShapeDtypeStruct((t, D_MODEL), F32),
        grid=(t // ROW_TILE,),
        in_specs=[pl.BlockSpec((ROW_TILE, D_MODEL), lambda i: (i, 0)),
                  pl.BlockSpec((1, D_MODEL), lambda i: (0, 0))],
        out_specs=pl.BlockSpec((ROW_TILE, D_MODEL), lambda i: (i, 0)),
        compiler_params=_cparams(1), name="final_norm",
    )(x, g.reshape(1, D_MODEL))


def _mm_kernel(a_ref, w_ref, o_ref):
    o_ref[...] = _dot(a_ref[...], w_ref[...])


def _matmul(a, w, tm, tn):
    m, k = a.shape
    n = w.shape[1]
    return pl.pallas_call(
        _mm_kernel,
        out_shape=jax.ShapeDtypeStruct((m, n), F32),
        grid=(n // tn, m // tm),
        in_specs=[pl.BlockSpec((tm, k), lambda j, i: (i, 0)),
                  pl.BlockSpec((k, tn), lambda j, i: (0, j))],
        out_specs=pl.BlockSpec((tm, tn), lambda j, i: (i, j)),
        compiler_params=_cparams(2), name="proj_matmul",
    )(a, w)


def _mix_out_kernel(att_ref, rw_ref, wa_ref, wb_ref, x_ref, gt_ref, o_ref, *, grp):
    rows = x_ref.shape[0]
    mix = _dot(att_ref[...], wa_ref[...]) + _dot(rw_ref[...], wb_ref[...])
    oh = _seq_onehot(pl.program_id(1) * rows, rows, grp)
    o_ref[...] = x_ref[...] + _dot_exact_lhs(oh, gt_ref[...]) * mix


def _mix_out(att, rw, w_out, x, mod, k_gt, grp):
    t = x.shape[0]
    tm, tn = ROW_TILE, 1024
    nb = D_MODEL // tn
    return pl.pallas_call(
        functools.partial(_mix_out_kernel, grp=grp),
        out_shape=jax.ShapeDtypeStruct((t, D_MODEL), F32),
        grid=(nb, t // tm),
        in_specs=[pl.BlockSpec((tm, C_HEADS), lambda j, i: (i, 0)),
                  pl.BlockSpec((tm, C_HEADS), lambda j, i: (i, 0)),
                  pl.BlockSpec((C_HEADS, tn), lambda j, i: (0, j)),
                  pl.BlockSpec((C_HEADS, tn), lambda j, i: (1, j)),
                  pl.BlockSpec((tm, tn), lambda j, i: (i, j)),
                  pl.BlockSpec((SEQ_TABLE_ROWS, tn), lambda j, i, k=k_gt, nb=nb: (0, k * nb + j))],
        out_specs=pl.BlockSpec((tm, tn), lambda j, i: (i, j)),
        compiler_params=_cparams(2), name="mix_out",
    )(att, rw, w_out, w_out, x, mod)


def _pattn_kernel(q_ref, k_ref, v_ref, o_ref, kmean):
    blk = q_ref.shape[0]
    n_blk = k_ref.shape[0] // blk
    n_pair = q_ref.shape[1] // LANES
    n_pad = kmean.shape[1]
    qi = pl.program_id(2)
    lane = lax.broadcasted_iota(I32, (blk, LANES), 1)
    low = lane < HEAD_DIM
    heads = [(pr, e) for pr in range(n_pair) for e in range(2)]

    @pl.when(qi == 0)
    def _():
        for pr in range(n_pair):
            rows = [jnp.sum(k_ref[n * blk:(n + 1) * blk, pr * LANES:(pr + 1) * LANES], axis=0, keepdims=True)
                    * (1.0 / blk) for n in range(n_blk)]
            if n_pad > n_blk:
                rows.append(jnp.zeros((n_pad - n_blk, LANES), F32))
            kmean[pr] = jnp.concatenate(rows, axis=0)

    blk_i = lax.broadcasted_iota(I32, (n_pad, blk), 0)
    past = jnp.where(blk_i < qi, 1.0, 0.0)
    qh, selm = [], []
    for pr, e in heads:
        q = q_ref[:, pr * LANES:(pr + 1) * LANES] * (HEAD_DIM ** -0.5)
        qm = jnp.where(low, q, 0.0) if e == 0 else jnp.where(low, 0.0, q)
        qh.append(qm.astype(BF16))
        s = _dot_nt(kmean[pr].astype(BF16), qh[-1])
        cnt = jnp.zeros((n_pad, blk), F32)
        for m in range(n_blk):
            sm = s[m:m + 1, :]
            beats = (sm > s) | ((sm == s) & (m < blk_i))
            cnt = cnt + jnp.where(beats, 1.0, 0.0) * jnp.where(m < qi, 1.0, 0.0)
        sel_t = jnp.where(cnt < MOBA_TOPK, past, 0.0)
        sel_t = jnp.concatenate([sel_t, jnp.zeros((LANES - n_pad, blk), F32)], axis=0)
        selm.append(jnp.transpose(sel_t))
    row_i = lax.broadcasted_iota(I32, (blk, blk), 0)
    col_i = lax.broadcasted_iota(I32, (blk, blk), 1)
    causal = jnp.where(col_i <= row_i, 1.0, 0.0)

    def body(n, carry):
        off = pl.multiple_of(n * blk, blk)
        own = jnp.where(n == qi, 1.0, 0.0)
        out = []
        for h, (pr, e) in enumerate(heads):
            kb = k_ref[pl.ds(off, blk), pr * LANES:(pr + 1) * LANES].astype(BF16)
            vb = v_ref[pl.ds(off, blk), pr * LANES:(pr + 1) * LANES].astype(BF16)
            m_old, l_old, acc = carry[3 * h:3 * h + 3]
            logits = _dot_nt(qh[h], kb)
            flag = jnp.sum(jnp.where(lane == n, selm[h], 0.0), axis=1, keepdims=True)
            allowed = (own * causal + (1.0 - own) * flag) > 0.5
            masked = jnp.where(allowed, logits, NEG_BIG)
            m_new = jnp.maximum(m_old, jnp.max(masked, axis=1, keepdims=True))
            alpha = jnp.exp(m_old - m_new)
            p = jnp.where(allowed, jnp.exp(masked - m_new), 0.0)
            l_new = l_old * alpha + jnp.sum(p, axis=1, keepdims=True)
            acc = acc * alpha + _dot(p.astype(BF16), vb)
            out += [m_new, l_new, acc]
        return tuple(out)

    init = (jnp.full((blk, 1), NEG_BIG, F32), jnp.zeros((blk, 1), F32),
            jnp.zeros((blk, LANES), F32)) * len(heads)
    res = lax.fori_loop(0, qi + 1, body, init)
    for pr in range(n_pair):
        a, b = 6 * pr, 6 * pr + 3
        o_ref[:, pr * LANES:(pr + 1) * LANES] = jnp.where(
            low, res[a + 2] / res[a + 1], res[b + 2] / res[b + 1]).astype(o_ref.dtype)


def _prompt_attention(qkv, n_seq, seq_len):
    blk = MOBA_BLOCK
    nq = seq_len // blk
    npair = PROMPT_PAIRS_PER_STEP
    w = npair * LANES
    ng = N_PAIRS // npair
    n_pad = -(-nq // SUBLANES) * SUBLANES
    assert n_pad <= LANES
    return pl.pallas_call(
        _pattn_kernel,
        out_shape=jax.ShapeDtypeStruct((n_seq * seq_len, C_HEADS), BF16),
        grid=(n_seq, ng, nq),
        in_specs=[pl.BlockSpec((blk, w), lambda b, p, i: (b * nq + i, p)),
                  pl.BlockSpec((seq_len, w), lambda b, p, i: (b, ng + p)),
                  pl.BlockSpec((seq_len, w), lambda b, p, i: (b, 2 * ng + p))],
        out_specs=pl.BlockSpec((blk, w), lambda b, p, i: (b * nq + i, p)),
        scratch_shapes=[pltpu.VMEM((npair, n_pad, LANES), F32)],
        compiler_params=_cparams(3), name="moba_prompt",
    )(qkv, qkv, qkv)


def _sattn_k_kernel(pt_ref, qr_ref, *refs, n_grp):
    kp_refs, (lg_ref, sc_ref) = refs[:n_grp], refs[n_grp:]
    j = pl.program_id(1)
    qr = qr_ref[0]
    qb = qr.astype(BF16)
    pages_per_block = MOBA_BLOCK // PAGE_SIZE
    blocks_per_step = n_grp // pages_per_block
    lane = lax.broadcasted_iota(I32, sc_ref.shape[1:], 1)

    @pl.when(j == 0)
    def _():
        sc_ref[0] = jnp.zeros(sc_ref.shape[1:], F32)

    kts = [ref[0] for ref in kp_refs]
    for u in range(n_grp):
        lg_ref[0, u] = _dot(qb, kts[u].astype(BF16))
    sc = sc_ref[0]
    for n in range(blocks_per_step):
        ksum = kts[n * pages_per_block]
        for u in range(1, pages_per_block):
            ksum = ksum + kts[n * pages_per_block + u]
        kmean = jnp.sum(ksum, axis=1, keepdims=True) * (1.0 / MOBA_BLOCK)
        s = _dot(qb, jnp.broadcast_to(kmean, (C_HEADS, LANES)).astype(BF16))
        sc = jnp.where(lane == j * blocks_per_step + n, s, sc)
    sc_ref[0] = sc


def _sattn_v_kernel(pt_ref, lg_ref, sc_ref, *refs, n_new, n_grp, n_blocks):
    vp_refs = refs[:n_grp]
    qr_ref, kn_ref, vn_ref, o_ref, selm, m_s, l_s, acc = refs[n_grp:]
    j = pl.program_id(1)
    nr = lg_ref.shape[2]
    lane = lax.broadcasted_iota(I32, (nr, LANES), 1)
    pages_per_block = MOBA_BLOCK // PAGE_SIZE

    @pl.when(j == 0)
    def _():
        s = sc_ref[0]
        cnt = jnp.zeros((nr, LANES), F32)
        for m in range(n_blocks):
            sm = s[:, m:m + 1]
            cnt = cnt + jnp.where((sm > s) | ((sm == s) & (m < lane)), 1.0, 0.0)
        selm[...] = jnp.where((cnt < MOBA_TOPK) & (lane < n_blocks), 1.0, 0.0)
        m_s[...] = jnp.full(m_s.shape, NEG_BIG, F32)
        l_s[...] = jnp.zeros(l_s.shape, F32)
        acc[...] = jnp.zeros(acc.shape, F32)

    def accumulate(logits, allowed, pv_fns):
        masked = [jnp.where(a, x, NEG_BIG) for x, a in zip(logits, allowed)]
        m_old = m_s[...]
        m_new = m_old
        for x in masked:
            m_new = jnp.maximum(m_new, jnp.max(x, axis=1, keepdims=True))
        alpha = jnp.exp(m_old - m_new)
        l_new = l_s[...] * alpha
        pv = acc[...] * alpha
        for x, a, fn in zip(masked, allowed, pv_fns):
            p = jnp.where(a, jnp.exp(x - m_new), 0.0)
            l_new = l_new + jnp.sum(p, axis=1, keepdims=True)
            pv = pv + fn(p.astype(BF16))
        l_s[...] = l_new
        acc[...] = pv
        m_s[...] = m_new

    sel = selm[...]
    flags = [jnp.sum(jnp.where(lane == (j * n_grp + u) // pages_per_block, sel, 0.0), axis=1, keepdims=True)
             for u in range(n_grp)]
    accumulate([lg_ref[0, u] for u in range(n_grp)],
               [jnp.broadcast_to(f, (nr, LANES)) > 0.0 for f in flags],
               [functools.partial(lambda p, ref: _dot_nt(p, ref[0].astype(BF16)), ref=ref) for ref in vp_refs])

    @pl.when(j == pl.num_programs(1) - 1)
    def _():
        pad = jnp.zeros((LANES - n_new, C_HEADS), F32)
        kn = jnp.concatenate([kn_ref[...], pad], axis=0).astype(BF16)
        vn = jnp.concatenate([vn_ref[...], pad], axis=0).astype(BF16)
        logits = _dot_nt(qr_ref[0].astype(BF16), kn)
        row = lax.broadcasted_iota(I32, (nr, LANES), 0)
        accumulate([logits], [(lane <= (row & (n_new - 1))) & (lane < n_new)], [lambda p: _dot(p, vn)])
        out = acc[...] / l_s[...]
        r2 = lax.broadcasted_iota(I32, (nr, C_HEADS), 0)
        c2 = lax.broadcasted_iota(I32, (nr, C_HEADS), 1)
        own = jnp.where((r2 >> _log2(n_new)) == (c2 >> _log2(HEAD_DIM)), out, 0.0)
        o_ref[...] = jnp.sum(own.reshape(N_HEADS, n_new, C_HEADS), axis=0)


def _sample_attention(qkv, row0, n_seq, n_new, cache_kt, cache_vt, page_table, page_base):
    n_pages = page_table.shape[1]
    n_grp = SAMPLE_PAGES_PER_STEP
    n_blocks = (n_pages * PAGE_SIZE) // MOBA_BLOCK
    assert n_pages % n_grp == 0 and n_grp % (MOBA_BLOCK // PAGE_SIZE) == 0 and n_blocks <= LANES
    n_steps = n_pages // n_grp
    nr = N_HEADS * n_new
    q = qkv[row0:row0 + n_seq * n_new, :C_HEADS].reshape(n_seq, 1, n_new, C_HEADS) * (HEAD_DIM ** -0.5)
    head_of_col = (jnp.arange(C_HEADS, dtype=I32) // HEAD_DIM)[None, None, None, :]
    head_of_row = jnp.arange(N_HEADS, dtype=I32)[None, :, None, None]
    qrows = jnp.where(head_of_row == head_of_col, q, 0.0).reshape(n_seq, nr, C_HEADS)
    pt = (page_table + page_base).astype(I32)
    page_specs = [pl.BlockSpec((1, C_HEADS, PAGE_SIZE), lambda b, j, pt, u=u: (pt[b, j * n_grp + u], 0, 0))
                  for u in range(n_grp)]
    lg_spec = pl.BlockSpec((1, n_grp, nr, PAGE_SIZE), lambda b, j, pt: (b, j, 0, 0))
    sc_spec = pl.BlockSpec((1, nr, LANES), lambda b, j, pt: (b, 0, 0))
    qr_spec = pl.BlockSpec((1, nr, C_HEADS), lambda b, j, pt: (b, 0, 0))

    logits, scores = pl.pallas_call(
        functools.partial(_sattn_k_kernel, n_grp=n_grp),
        out_shape=(jax.ShapeDtypeStruct((n_seq, n_pages, nr, PAGE_SIZE), F32),
                   jax.ShapeDtypeStruct((n_seq, nr, LANES), F32)),
        grid_spec=pltpu.PrefetchScalarGridSpec(
            num_scalar_prefetch=1, grid=(n_seq, n_steps),
            in_specs=[qr_spec] + page_specs,
            out_specs=(lg_spec, sc_spec)),
        compiler_params=_cparams(2), name="moba_sample_k",
    )(pt, qrows, *([cache_kt] * n_grp))

    rb = row0 // n_new
    return pl.pallas_call(
        functools.partial(_sattn_v_kernel, n_new=n_new, n_grp=n_grp, n_blocks=n_blocks),
        out_shape=jax.ShapeDtypeStruct((n_seq * n_new, C_HEADS), F32),
        grid_spec=pltpu.PrefetchScalarGridSpec(
            num_scalar_prefetch=1, grid=(n_seq, n_steps),
            in_specs=[lg_spec, sc_spec] + page_specs + [
                qr_spec,
                pl.BlockSpec((n_new, C_HEADS), lambda b, j, pt: (rb + b, 1)),
                pl.BlockSpec((n_new, C_HEADS), lambda b, j, pt: (rb + b, 2))],
            out_specs=pl.BlockSpec((n_new, C_HEADS), lambda b, j, pt: (b, 0)),
            scratch_shapes=[pltpu.VMEM((nr, LANES), F32), pltpu.VMEM((nr, 1), F32),
                            pltpu.VMEM((nr, 1), F32), pltpu.VMEM((nr, C_HEADS), F32)]),
        compiler_params=_cparams(2), name="moba_sample_v",
    )(pt, logits, scores, *([cache_vt] * n_grp), qrows, qkv, qkv)


def _rwkv_prep_kernel(*refs, has_vres, grp):
    if has_vres:
        (pr_ref, edge_ref, s0_ref, mu_ref, w0_ref, a0_ref, wup_ref, aup_ref, gup_ref, kk_ref_, ka_ref,
         e_ref, vf_ref, v0_ref, vdn_ref, vup_ref,
         r_o, w_o, lw_o, k_o, v_o, kk_o, b_o, g_o) = refs
    else:
        (pr_ref, edge_ref, s0_ref, mu_ref, w0_ref, a0_ref, wup_ref, aup_ref, gup_ref, kk_ref_, ka_ref,
         e_ref, r_o, w_o, lw_o, k_o, v_o, kk_o, b_o, g_o) = refs
    n_prompt_rows, tp_shift, ts_shift, _ = grp
    pr = pr_ref[...]
    tm = pr.shape[0]
    loc = lax.broadcasted_iota(I32, (tm, 1), 0)
    glob = pl.program_id(0) * tm + loc
    prev = jnp.where(loc == 0, edge_ref[SUBLANES - 1:SUBLANES, :], pltpu.roll(pr, 1, 0))
    first_p = (glob < n_prompt_rows) & ((glob & ((1 << tp_shift) - 1)) == 0)
    first_s = (glob >= n_prompt_rows) & (((glob - n_prompt_rows) & ((1 << ts_shift) - 1)) == 0)
    prev = jnp.where(first_p, 0.0, jnp.where(first_s, s0_ref[...], prev))
    xm = pr + (prev - pr) * mu_ref[...]
    c = C_HEADS
    r, k, v, lora = xm[:, :c], xm[:, c:2 * c], xm[:, 2 * c:3 * c], xm[:, 3 * c:]
    z = w0_ref[...] + _dot_ref(jnp.tanh(lora), wup_ref[...])
    w_log = -(jnp.maximum(-z, 0.0) + jnp.log(1.0 + jnp.exp(-jnp.abs(z)))) - 0.5
    log_decay = -jnp.exp(w_log)
    decay = jnp.exp(log_decay)
    a = _sigmoid(a0_ref[...] + _dot_ref(lora, aup_ref[...]))
    g = _dot_ref(_sigmoid(lora), gup_ref[...])
    if has_vres:
        gate = _sigmoid(v0_ref[...] + _dot_ref(_dot_ref(v, vdn_ref[...]), vup_ref[...]))
        v = v + (vf_ref[...] - v) * gate
    kk = k * kk_ref_[...]
    ss = _head_segsum(kk * kk, e_ref)
    kk = kk * lax.rsqrt(jnp.maximum(ss, 1e-24))
    r_o[...] = r
    w_o[...] = decay
    lw_o[...] = log_decay
    k_o[...] = k * (1.0 + (a - 1.0) * ka_ref[...])
    v_o[...] = v
    kk_o[...] = kk
    b_o[...] = kk * a
    g_o[...] = g


def _rwkv_prep(pr, shift_rows, vecs, mats, e128, vres, grp):
    t = pr.shape[0]
    tm = 128
    c = C_HEADS
    assert grp[0] % tm == 0 and (t - grp[0]) % tm == 0
    n_prompt_tiles = grp[0] // tm
    row = lambda w: pl.BlockSpec((tm, w), lambda i: (i, 0))
    vec = lambda w: pl.BlockSpec((1, w), lambda i: (0, 0))
    full = lambda a: pl.BlockSpec(a.shape, lambda i: (0, 0))
    edge = pl.BlockSpec((SUBLANES, C_SHIFT_PAD), lambda i: (jnp.maximum(i * (tm // SUBLANES) - 1, 0), 0))
    s0 = pl.BlockSpec((tm, C_SHIFT_PAD), lambda i: (jnp.maximum(i - n_prompt_tiles, 0), 0))
    mu, w0, a0, k_k, k_a = vecs
    wup, aup, gup = mats
    args = [pr, pr, shift_rows, mu, w0, a0, wup, aup, gup, k_k, k_a, e128]
    specs = [row(C_SHIFT_PAD), edge, s0, vec(C_SHIFT_PAD), vec(c), vec(c),
             full(wup), full(aup), full(gup), vec(c), vec(c), full(e128)]
    if vres is not None:
        v_first, v0, vdn, vup = vres
        args += [v_first, v0, vdn, vup]
        specs += [row(c), vec(c), full(vdn), full(vup)]
    return pl.pallas_call(
        functools.partial(_rwkv_prep_kernel, has_vres=vres is not None, grp=grp),
        out_shape=tuple(jax.ShapeDtypeStruct((t, c), F32) for _ in range(8)),
        grid=(t // tm,),
        in_specs=specs,
        out_specs=tuple(row(c) for _ in range(8)),
        compiler_params=_cparams(1), name="rwkv_prep",
    )(*args)


def _rwkv_rec_kernel(r_ref, w_ref, k_ref, v_ref, kk_ref, b_ref, s0_ref, y_ref, st_ref, state):
    nb, tc = r_ref.shape[:2]
    c = pl.program_id(1)

    @pl.when(c == 0)
    def _():
        state[...] = s0_ref[...]

    lane = lax.broadcasted_iota(I32, (HEAD_DIM, LANES), 1)
    row = lax.broadcasted_iota(I32, (HEAD_DIM, LANES), 0)
    low = lane < HEAD_DIM
    eye = jnp.where(low, lane, lane - HEAD_DIM) == row

    def seg(x):
        e = jnp.sum(jnp.where(low, x, 0.0), axis=1, keepdims=True)
        o = jnp.sum(jnp.where(low, 0.0, x), axis=1, keepdims=True)
        return jnp.where(low, e, o)

    def step(t, carry):
        for bi in range(nb):
            r_t, w_t, k_t, v_t, kk_t, b_t = (ref[bi, t] for ref in
                                             (r_ref, w_ref, k_ref, v_ref, kk_ref, b_ref))
            y_rows = []
            for p in range(N_PAIRS):
                s_old = state[bi, p]
                sa = seg(s_old * (-kk_t[p:p + 1, :]))
                v_col = seg(jnp.where(eye, v_t[p:p + 1, :], 0.0))
                s_new = s_old * w_t[p:p + 1, :] + sa * b_t[p:p + 1, :] + v_col * k_t[p:p + 1, :]
                state[bi, p] = s_new
                y_b = seg(s_new * r_t[p:p + 1, :])
                y_rows.append(jnp.sum(jnp.where(eye, y_b, 0.0), axis=0, keepdims=True))
            y_ref[bi, t] = jnp.concatenate(y_rows, axis=0)
        return carry

    lax.fori_loop(0, tc, step, 0)

    @pl.when(c == pl.num_programs(1) - 1)
    def _():
        st_ref[...] = state[...]


def _rwkv_rec(seqs, s0, nb, tc):
    b, t = seqs[0].shape[:2]
    blk = pl.BlockSpec((nb, tc, N_PAIRS, LANES), lambda g, i: (g, i, 0, 0))
    st = pl.BlockSpec((nb, N_PAIRS, HEAD_DIM, LANES), lambda g, i: (g, 0, 0, 0))
    return pl.pallas_call(
        _rwkv_rec_kernel,
        out_shape=(jax.ShapeDtypeStruct((b, t, N_PAIRS, LANES), F32),
                   jax.ShapeDtypeStruct((b, N_PAIRS, HEAD_DIM, LANES), F32)),
        grid=(b // nb, t // tc),
        in_specs=[blk] * 6 + [st],
        out_specs=(blk, st),
        scratch_shapes=[pltpu.VMEM((nb, N_PAIRS, HEAD_DIM, LANES), F32)],
        compiler_params=_cparams(2), name="rwkv_rec",
    )(*seqs, s0)


def _sp(x):
    return _split2(x)


def _mm3(a, b, mode="nn"):
    if mode == "nt":
        d = _dot_nt
    elif mode == "tn":
        d = lambda x, y: lax.dot_general(x, y, (((0,), (0,)), ((), ())), preferred_element_type=F32)
    else:
        d = _dot
    return d(a[0], b[0]) + (d(a[0], b[1]) + d(a[1], b[0]))


def _rwkv_chunk_kernel(r_ref, lw_ref, k_ref, v_ref, kk_ref, b_ref, y_ref, st_ref, hbd, *, pairs):
    c = pl.program_id(2)
    n_c = r_ref.shape[0]
    n2 = 2 * n_c
    log_c = _log2(n_c)

    @pl.when(c == 0)
    def _():
        hbd[...] = jnp.zeros(hbd.shape, F32)

    lane = lax.broadcasted_iota(I32, (n_c, LANES), 1)
    low = lane < HEAD_DIM
    i = lax.broadcasted_iota(I32, (n2, n2), 0)
    j = lax.broadcasted_iota(I32, (n2, n2), 1)
    same = (i >> log_c) == (j >> log_c)
    ti = i & (n_c - 1)
    tj = j & (n_c - 1)
    m_stril = jnp.where(same & (tj < ti), 1.0, 0.0)
    m_tril = jnp.where(same & (tj <= ti), 1.0, 0.0)
    eye = jnp.where(i == j, 1.0, 0.0)
    m_blk = {s: jnp.where((i >> s) == (j >> s), 1.0, 0.0) for s in range(3, log_c + 1)}
    ci = lax.broadcasted_iota(I32, (n_c, n_c), 0)
    cj = lax.broadcasted_iota(I32, (n_c, n_c), 1)
    tri_c = jnp.where(cj <= ci, 1.0, 0.0).astype(BF16)

    def stack(x):
        return jnp.concatenate([jnp.where(low, x, 0.0), jnp.where(low, 0.0, x)], axis=0)

    def each(fn, *cols):
        return [fn(*args) for args in zip(*cols)]

    sls = [slice(q * LANES, (q + 1) * LANES) for q in range(pairs)]
    lw = [lw_ref[:, sl] for sl in sls]
    cum = each(lambda x: _dot_exact_lhs(tri_c, x), lw)
    cum_c = each(lambda x: x[n_c - 1:n_c, :], cum)
    e_neg = each(lambda x: jnp.exp(-x), cum)
    e_tail = each(lambda x, xc: jnp.exp(xc - x), cum, cum_c)
    kk = [kk_ref[:, sl] for sl in sls]
    b = [b_ref[:, sl] for sl in sls]
    k = [k_ref[:, sl] for sl in sls]
    a_s = each(lambda x, cu, l: _sp(stack(-x * jnp.exp(cu - l))), kk, cum, lw)
    b_s = each(lambda x, e: _sp(stack(x * e)), b, e_neg)
    k_s = each(lambda x, e: _sp(stack(x * e)), k, e_neg)
    r_st = [stack(r_ref[:, sl] * jnp.exp(cu)) for sl, cu in zip(sls, cum)]
    r_s = each(_sp, r_st)
    bh_s = each(lambda x, e: _sp(stack(x * e)), b, e_tail)
    kh_s = each(lambda x, e: _sp(stack(x * e)), k, e_tail)
    v_s = [_sp(stack(v_ref[:, sl])) for sl in sls]

    l_ab = each(lambda x, y: m_stril * _mm3(x, y, "nt"), a_s, b_s)
    l_ak = each(lambda x, y: _sp(m_stril * _mm3(x, y, "nt")), a_s, k_s)
    m_rb = each(lambda x, y: _sp(m_tril * _mm3(x, y, "nt")), r_s, b_s)
    m_rk = each(lambda x, y: _sp(m_tril * _mm3(x, y, "nt")), r_s, k_s)

    d1 = each(lambda x: x * m_blk[3], l_ab)
    d1s = each(_sp, d1)
    d2 = each(lambda x: _mm3(x, x), d1s)
    d2s = each(_sp, d2)
    d4 = each(lambda x: _mm3(x, x), d2s)
    i12 = each(lambda x, y: _sp(_mm3(_sp(eye + x), _sp(eye + y))), d1, d2)
    inv = each(lambda x, y: _mm3(x, _sp(eye + y)), i12, d4)
    for s in range(4, log_c + 1):
        l_m = each(lambda x: _sp(x * (m_blk[s] - m_blk[s - 1])), l_ab)
        inv_s = each(_sp, inv)
        t_m = each(lambda x, y: _sp(_mm3(x, y)), inv_s, l_m)
        inv = each(lambda x, y, z: x + _mm3(y, z), inv, t_m, inv_s)
    inv_s = each(_sp, inv)
    w1 = each(lambda x, y: _sp(_mm3(x, y)), inv_s, a_s)
    lv = each(lambda x, y: _sp(_mm3(x, y)), l_ak, v_s)
    u0 = each(lambda x, y: _sp(_mm3(x, y)), inv_s, lv)
    y1 = each(lambda x, y, z: x + _mm3(y, z), r_st, m_rb, w1)
    y0 = each(lambda x, y, z, u: _mm3(x, y) + _mm3(z, u), m_rb, u0, m_rk, v_s)
    g = each(lambda xc, x, y: _sp(eye * jnp.exp(xc) + _mm3(x, y, "tn")), cum_c, bh_s, w1)
    h_add = each(lambda x, y, z, u: _mm3(x, y, "tn") + _mm3(z, u, "tn"), bh_s, u0, kh_s, v_s)
    y1c = each(lambda x: _sp(x[:n_c] + x[n_c:]), y1)
    y0c = each(lambda x: x[:n_c] + x[n_c:], y0)

    h_s = [_sp(hbd[q]) for q in range(pairs)]
    for q in range(pairs):
        y_ref[:, sls[q]] = _mm3(y1c[q], h_s[q]) + y0c[q]
    for q in range(pairs):
        hbd[q] = _mm3(g[q], h_s[q]) + h_add[q]

    @pl.when(c == pl.num_programs(2) - 1)
    def _():
        st_ref[0] = hbd[...]


def _rwkv_chunked(seqs, n_seq, seq_len, chunk, pairs):
    n_chunks = seq_len // chunk
    n_grp = N_PAIRS // pairs
    blk = pl.BlockSpec((chunk, pairs * LANES), lambda b, p, c: (b * n_chunks + c, p))
    return pl.pallas_call(
        functools.partial(_rwkv_chunk_kernel, pairs=pairs),
        out_shape=(jax.ShapeDtypeStruct((n_seq * seq_len, C_HEADS), F32),
                   jax.ShapeDtypeStruct((n_seq * n_grp, pairs, LANES, LANES), F32)),
        grid=(n_seq, n_grp, n_chunks),
        in_specs=[blk] * 6,
        out_specs=(blk, pl.BlockSpec((1, pairs, LANES, LANES), lambda b, p, c: (b * n_grp + p, 0, 0, 0))),
        scratch_shapes=[pltpu.VMEM((pairs, LANES, LANES), F32)],
        compiler_params=_cparams(3), name="rwkv_chunked",
    )(*seqs)


def _unpack_hbd(h, n_seq):
    h = h.reshape(n_seq, N_PAIRS, 2, HEAD_DIM, 2, HEAD_DIM)
    diag = jnp.stack([h[:, :, 0, :, 0, :], h[:, :, 1, :, 1, :]], axis=2)
    return jnp.transpose(diag, (0, 1, 2, 4, 3)).reshape(n_seq, N_HEADS, HEAD_DIM, HEAD_DIM)


def _rwkv_post_kernel(y_ref, r_ref, k_ref, v_ref, g_ref, lnw_ref, lnb_ref, rk_ref, e_ref, o_ref):
    y = y_ref[...]
    inv = 1.0 / HEAD_DIM
    mean = _head_segsum(y, e_ref) * inv
    yc = y - mean
    var = _head_segsum(yc * yc, e_ref) * inv
    yn = yc * lax.rsqrt(var + LN_X_EPS) * lnw_ref[...] + lnb_ref[...]
    bonus = _head_segsum(r_ref[...] * k_ref[...] * rk_ref[...], e_ref) * v_ref[...]
    o_ref[...] = ((yn + bonus) * g_ref[...]).astype(o_ref.dtype)


def _rwkv_post(y, r, k, v, g, ln_w, ln_b, r_k, e128):
    t, c = y.shape
    row = pl.BlockSpec((ROW_TILE, c), lambda i: (i, 0))
    vec = pl.BlockSpec((1, c), lambda i: (0, 0))
    return pl.pallas_call(
        _rwkv_post_kernel,
        out_shape=jax.ShapeDtypeStruct((t, c), BF16),
        grid=(t // ROW_TILE,),
        in_specs=[row] * 5 + [vec] * 3 + [pl.BlockSpec(e128.shape, lambda i: (0, 0))],
        out_specs=row,
        compiler_params=_cparams(1), name="rwkv_post",
    )(y, r, k, v, g, ln_w, ln_b, r_k, e128)


def _pack_state(s):
    b = s.shape[0]
    s = s.reshape(b, N_PAIRS, 2, HEAD_DIM, HEAD_DIM)
    return jnp.transpose(s, (0, 1, 3, 2, 4)).reshape(b, N_PAIRS, HEAD_DIM, LANES)


def _unpack_state(s):
    b = s.shape[0]
    s = s.reshape(b, N_PAIRS, HEAD_DIM, 2, HEAD_DIM)
    return jnp.transpose(s, (0, 1, 3, 2, 4)).reshape(b, N_HEADS, HEAD_DIM, HEAD_DIM)


def _expert_kernel(src_ref, te_ref, nu_ref, h_ref, w1_ref, b1_ref, w2_ref, b2_ref, sel_ref, o_ref,
                   xbuf, xb, sems):
    t = pl.program_id(0)
    j = pl.program_id(1)
    nf = pl.num_programs(1)
    tm = o_ref.shape[0]
    per_step = tm // nf

    def start_row(tile, slot, r):
        pltpu.make_async_copy(h_ref.at[pl.ds(src_ref[tile * tm + r], 1), :],
                              xbuf.at[slot, pl.ds(r, 1), :], sems.at[slot]).start(priority=1)

    @pl.when(t < nu_ref[0])
    def _():
        @pl.when((t == 0) & (j == 0))
        def _():
            def first(r, carry):
                start_row(0, 0, r)
                return carry
            lax.fori_loop(0, tm, first, 0)

        def wait_tile(tile):
            slot = tile % 2
            pltpu.make_async_copy(h_ref.at[pl.ds(0, tm), :], xbuf.at[slot], sems.at[slot]).wait()

        @pl.when(j == 0)
        def _():
            wait_tile(t)
            xb[...] = xbuf[t % 2].astype(BF16)

        nxt_tile = jnp.minimum(t + 1, nu_ref[0] - 1)
        for r in range(per_step):
            start_row(nxt_tile, (t + 1) % 2, j * per_step + r)

        w1b = w1_ref[0].astype(BF16)
        w2b = w2_ref[0].astype(BF16)
        sub = tm // EXPERT_SUBTILES
        rows = [slice(s * sub, (s + 1) * sub) for s in range(EXPERT_SUBTILES)]
        hh = [_dot(xb[rs, :], w1b) + b1_ref[0] for rs in rows]
        width = hh[0].shape[1]
        nxt = [pltpu.roll(h, width - 1, 1) for h in hh]
        glu = [jnp.minimum(h, SWIGLU_LIMIT) for h in hh]
        act = [g * _sigmoid(SWIGLU_ALPHA * g) * (jnp.clip(n, -SWIGLU_LIMIT, SWIGLU_LIMIT) + 1.0)
               for g, n in zip(glu, nxt)]
        act = [_dot(a.astype(BF16), sel_ref[...]).astype(BF16) for a in act]
        part = [_dot(a, w2b) for a in act]

        @pl.when(j == 0)
        def _():
            for rs, p in zip(rows, part):
                o_ref[rs, :] = p + b2_ref[0]

        @pl.when(j > 0)
        def _():
            for rs, p in zip(rows, part):
                o_ref[rs, :] = o_ref[rs, :] + p

        @pl.when((t == nu_ref[0] - 1) & (j == nf - 1))
        def _():
            wait_tile(t + 1)

    @pl.when((t >= nu_ref[0]) & (j == 0))
    def _():
        o_ref[...] = jnp.zeros(o_ref.shape, F32)


def _expert_mlp(h, src, tile_expert, n_used, w1, b1, w2, b2):
    n_rows = src.shape[0]
    tm, bf = EXPERT_TILE, EXPERT_FBLOCK
    n_tiles = n_rows // tm
    nf = D_EXPERT // bf
    sel = (jnp.arange(2 * bf, dtype=I32)[:, None] == 2 * jnp.arange(bf, dtype=I32)[None, :]).astype(BF16)
    return pl.pallas_call(
        _expert_kernel,
        out_shape=jax.ShapeDtypeStruct((n_rows, D_MODEL), F32),
        grid_spec=pltpu.PrefetchScalarGridSpec(
            num_scalar_prefetch=3, grid=(n_tiles, nf),
            in_specs=[pl.BlockSpec(memory_space=pl.ANY),
                      pl.BlockSpec((1, D_MODEL, 2 * bf), lambda t, j, src, te, nu: (te[t], 0, j)),
                      pl.BlockSpec((1, 1, 2 * bf), lambda t, j, src, te, nu: (te[t], 0, j)),
                      pl.BlockSpec((1, bf, D_MODEL), lambda t, j, src, te, nu: (te[t], j, 0)),
                      pl.BlockSpec((1, 1, D_MODEL), lambda t, j, src, te, nu: (te[t], 0, 0)),
                      pl.BlockSpec((2 * bf, bf), lambda t, j, src, te, nu: (0, 0))],
            out_specs=pl.BlockSpec((tm, D_MODEL), lambda t, j, src, te, nu: (t, 0)),
            scratch_shapes=[pltpu.VMEM((2, tm, D_MODEL), F32), pltpu.VMEM((tm, D_MODEL), BF16),
                            pltpu.SemaphoreType.DMA((2,))]),
        compiler_params=_cparams(2), name="expert_mlp",
    )(src, tile_expert, n_used, h, w1.reshape(-1, D_MODEL, 2 * D_EXPERT), b1.reshape(-1, 1, 2 * D_EXPERT),
      w2.reshape(-1, D_EXPERT, D_MODEL), b2.reshape(-1, 1, D_MODEL), sel)


def _combine_kernel(slot_ref, ys_ref, x_ref, route_ref, gt_ref, o_ref, ybuf, sem, *, grp):
    rows = x_ref.shape[0]
    base = pl.program_id(0) * rows

    def copy(r, j):
        return pltpu.make_async_copy(ys_ref.at[pl.ds(slot_ref[(base + r) * TOP_K + j], 1), :],
                                     ybuf.at[j, pl.ds(r, 1), :], sem)

    def start(r, carry):
        for j in range(TOP_K):
            copy(r, j).start(priority=j % 2)
        return carry

    def wait(r, carry):
        for j in range(TOP_K):
            copy(r, j).wait()
        return carry

    lax.fori_loop(0, rows, start, 0, unroll=4)
    lax.fori_loop(0, rows, wait, 0, unroll=4)
    route = route_ref[...]
    moe = jnp.zeros(x_ref.shape, F32)
    for j in range(TOP_K):
        moe = moe + route[:, TOP_K + j:TOP_K + j + 1] * ybuf[j]
    oh = _seq_onehot(base, rows, grp)
    o_ref[...] = x_ref[...] + _dot_exact_lhs(oh, gt_ref[...]) * moe


def _combine(x, ys, slots, route, mod, k_gt, grp):
    t = x.shape[0]
    tm = 128
    return pl.pallas_call(
        functools.partial(_combine_kernel, grp=grp),
        out_shape=jax.ShapeDtypeStruct((t, D_MODEL), F32),
        grid_spec=pltpu.PrefetchScalarGridSpec(
            num_scalar_prefetch=1, grid=(t // tm,),
            in_specs=[pl.BlockSpec(memory_space=pl.ANY),
                      pl.BlockSpec((tm, D_MODEL), lambda i, s: (i, 0)),
                      pl.BlockSpec((tm, LANES), lambda i, s: (i, 0)),
                      pl.BlockSpec((SEQ_TABLE_ROWS, D_MODEL), lambda i, s, k=k_gt: (0, k))],
            out_specs=pl.BlockSpec((tm, D_MODEL), lambda i, s: (i, 0)),
            scratch_shapes=[pltpu.VMEM((TOP_K, tm, D_MODEL), F32), pltpu.SemaphoreType.DMA]),
        compiler_params=_cparams(1), name="moe_combine",
    )(slots, ys, x, route, mod)


def _moe_layer(x, g, mod, wr_pad, br_pad, w1, b1, w2, b2, layer, grp):
    t = x.shape[0]
    h2, route, counts = _rms_router(x, g, mod, 4, 3, wr_pad, br_pad, grp)
    tm = EXPERT_TILE
    n_tiles = (t * TOP_K) // tm + N_EXPERTS
    cnt = counts[0, :N_EXPERTS].astype(I32)
    tiles_per = (cnt + tm - 1) // tm
    tile_end = jnp.cumsum(tiles_per)
    group_start = (tile_end - tiles_per) * tm
    idx = route[:, :TOP_K].astype(I32)
    pos = route[:, 2 * TOP_K:3 * TOP_K].astype(I32)
    slots = (group_start[idx] + pos).reshape(-1)
    token_of_pair = jnp.repeat(jnp.arange(t, dtype=I32), TOP_K)
    src = jnp.zeros((n_tiles * tm,), I32).at[slots].set(token_of_pair)
    n_used = tile_end[-1:].astype(I32)
    tile_expert = jnp.searchsorted(tile_end, jnp.arange(n_tiles, dtype=I32), side="right").astype(I32)
    tile_expert = jnp.minimum(tile_expert, N_EXPERTS - 1)
    tile_expert = jnp.where(jnp.arange(n_tiles) < n_used[0], tile_expert,
                            tile_expert[jnp.maximum(n_used[0] - 1, 0)])
    ys = _expert_mlp(h2, src, tile_expert + layer * N_EXPERTS, n_used, w1, b1, w2, b2)
    return _combine(x, ys, slots, route, mod, 5, grp)


def _log2(n):
    s = n.bit_length() - 1
    assert (1 << s) == n, "sequence lengths must be powers of two"
    return s


def kernel(x_prompt, x_sample, cache_k, cache_v, page_table, state_shift, state_wkv, c_prompt, c_sample,
           w_ada, b_ada, norm_mix, norm_ffn, w_in, w_out, rw_mu, rw_w0, rw_w_up, rw_a0, rw_a_up,
           rw_g_up, rw_k_k, rw_k_a, rw_r_k, rw_ln_w, rw_ln_b, rw_v0, rw_v_down, rw_v_up,
           w_router, b_router, w_mlp1, b_mlp1, w_mlp2, b_mlp2, norm_final):
    bp, tp, d = x_prompt.shape
    bs, ts, _ = x_sample.shape
    depth = w_ada.shape[0]
    n_pool = cache_k.shape[1]
    np_rows, ns_rows = bp * tp, bs * ts
    t_all = np_rows + ns_rows
    assert d == D_MODEL and t_all % ROW_TILE == 0 and np_rows % ROW_TILE == 0
    assert bp + bs <= SEQ_TABLE_ROWS and tp % MOBA_BLOCK == 0 and ts <= SUBLANES
    grp = (np_rows, _log2(tp), _log2(ts), bp)
    c = C_HEADS

    x = jnp.concatenate([x_prompt.reshape(np_rows, d), x_sample.reshape(ns_rows, d)], axis=0)
    c_pad = jnp.zeros((SEQ_TABLE_ROWS, d), F32).at[:bp + bs].set(jnp.concatenate([c_prompt, c_sample], axis=0))
    mod = _ada(c_pad, w_ada, b_ada)

    cache_kt = jnp.transpose(cache_k, (0, 1, 3, 4, 2)).reshape(depth * n_pool, c, PAGE_SIZE)
    cache_vt = jnp.transpose(cache_v, (0, 1, 3, 4, 2)).reshape(depth * n_pool, c, PAGE_SIZE)
    e128 = (jnp.arange(LANES)[:, None] // HEAD_DIM == jnp.arange(LANES)[None, :] // HEAD_DIM).astype(BF16)

    def lora_pad(w, start):
        return jnp.zeros((C_LORA_PAD, c), F32).at[start:start + w.shape[0]].set(w)

    k_out, v_out, shift_p, shift_s, wkv_p, wkv_s = [], [], [], [], [], []
    v_first = None
    for l in range(depth):
        h = _rms_mod(x, norm_mix[l], mod[l], 1, 0, grp)
        w_qkv = w_in[l][:, :3 * c].astype(BF16)
        w_pr = jnp.zeros((d, C_SHIFT_PAD), BF16).at[:, :C_SHIFT].set(w_in[l][:, 3 * c:].astype(BF16))
        qkv = _matmul(h, w_qkv, 768, 1024)
        pr = _matmul(h, w_pr, 768, 512)

        att_p = _prompt_attention(qkv, bp, tp)
        att_s = _sample_attention(qkv, np_rows, bs, ts, cache_kt, cache_vt, page_table, l * n_pool)
        att = jnp.concatenate([att_p, att_s.astype(BF16)], axis=0)

        shift_rows = jnp.zeros((bs, ts, C_SHIFT_PAD), F32).at[:, 0, :C_SHIFT].set(state_shift[l])
        shift_rows = shift_rows.reshape(ns_rows, C_SHIFT_PAD)
        mu =jnp.zeros((1, C_SHIFT_PAD), F32).at[0, :C_SHIFT].set(rw_mu[l])
        vecs = (mu, rw_w0[l].reshape(1, c), rw_a0[l].reshape(1, c), rw_k_k[l].reshape(1, c),
                rw_k_a[l].reshape(1, c))
        mats = (lora_pad(rw_w_up[l], 0), lora_pad(rw_a_up[l], D_DECAY_LORA),
                lora_pad(rw_g_up[l], D_DECAY_LORA + D_AAA_LORA))
        vres = None
        if l > 0:
            vdn = jnp.zeros((c, LANES), F32).at[:, :D_MV_LORA].set(rw_v_down[l - 1])
            vup = jnp.zeros((LANES, c), F32).at[:D_MV_LORA].set(rw_v_up[l - 1])
            vres = (v_first, rw_v0[l - 1].reshape(1, c), vdn, vup)
        r, w, lw, k, v, kk, b, g = _rwkv_prep(pr, shift_rows, vecs, mats, e128, vres, grp)
        if l == 0:
            v_first = v
        y_p, h_p = _rwkv_chunked((r, lw, k, v, kk, b), bp, tp, RWKV_CHUNK, RWKV_PAIRS_PER_STEP)
        seqs_s = [a[np_rows:].reshape(bs, ts, N_PAIRS, LANES) for a in (r, w, k, v, kk, b)]
        y_s, st_s = _rwkv_rec(seqs_s, _pack_state(state_wkv[l]), 4, ts)
        y = jnp.concatenate([y_p, y_s.reshape(ns_rows, c)], axis=0)
        rw = _rwkv_post(y, r, k, v, g, rw_ln_w[l].reshape(1, c), rw_ln_b[l].reshape(1, c),
                        rw_r_k[l].reshape(1, c), e128)

        x = _mix_out(att, rw, w_out[l].astype(BF16), x, mod[l], 2, grp)

        wr_pad = jnp.zeros((d, LANES), F32).at[:, :N_EXPERTS].set(w_router[l])
        br_pad = jnp.full((1, LANES), NEG_BIG, F32).at[0, :N_EXPERTS].set(b_router[l])
        x = _moe_layer(x, norm_ffn[l], mod[l], wr_pad, br_pad, w_mlp1, b_mlp1, w_mlp2, b_mlp2, l, grp)

        k_out.append(qkv[:, c:2 * c])
        v_out.append(qkv[:, 2 * c:3 * c])
        shift_p.append(pr[tp - 1:np_rows:tp, :C_SHIFT])
        shift_s.append(pr[np_rows + ts - 1::ts, :C_SHIFT])
        wkv_p.append(_unpack_hbd(h_p.reshape(bp * N_PAIRS, LANES, LANES), bp))
        wkv_s.append(_unpack_state(st_s))

    y = _final_norm(x, norm_final)
    k_all, v_all = jnp.stack(k_out), jnp.stack(v_out)
    pages = tp // PAGE_SIZE
    return (y[:np_rows].reshape(bp, tp, d), y[np_rows:].reshape(bs, ts, d),
            k_all[:, :np_rows].reshape(depth, bp, pages, PAGE_SIZE, N_HEADS, HEAD_DIM),
            v_all[:, :np_rows].reshape(depth, bp, pages, PAGE_SIZE, N_HEADS, HEAD_DIM),
            jnp.stack(shift_p), jnp.stack(wkv_p),
            k_all[:, np_rows:].reshape(depth, bs, ts, N_HEADS, HEAD_DIM),
            v_all[:, np_rows:].reshape(depth, bs, ts, N_HEADS, HEAD_DIM),
            jnp.stack(shift_s), jnp.stack(wkv_s))
```

```python
import functools

import jax
import jax.numpy as jnp
from jax import lax
from jax.experimental import pallas as pl
from jax.experimental.pallas import tpu as pltpu

F32, BF16, I32 = jnp.float32, jnp.bfloat16, jnp.int32

LANES = 128
SUBLANES = 8
VMEM_LIMIT = 50 * 1024 * 1024

D_MODEL = 2048
HEAD_DIM = 64
N_HEADS = 16
C_HEADS = N_HEADS * HEAD_DIM
N_PAIRS = C_HEADS // LANES
MOBA_BLOCK = 256
MOBA_TOPK = 3
PAGE_SIZE = 128
D_DECAY_LORA, D_AAA_LORA, D_GATE_LORA, D_MV_LORA = 64, 64, 160, 32
C_SHIFT = 3 * C_HEADS + D_DECAY_LORA + D_AAA_LORA + D_GATE_LORA
C_SHIFT_PAD = 3584
C_LORA_PAD = C_SHIFT_PAD - 3 * C_HEADS
LN_X_EPS = 64e-5
RMS_EPS = 1e-5
N_EXPERTS = 32
TOP_K = 4
D_EXPERT = D_MODEL
SWIGLU_LIMIT = 7.0
SWIGLU_ALPHA = 1.702
NEG_BIG = -3.0e38

ROW_TILE = 256
SEQ_TABLE_ROWS = 128
EXPERT_TILE = 576
EXPERT_FBLOCK = 256
GATHER_TILES_PER_STEP = 2
EXPERT_SUBTILES = 3
RWKV_CHUNK = 64
RWKV_PAIRS_PER_STEP = 8
SAMPLE_PAGES_PER_STEP = 8
PROMPT_PAIRS_PER_STEP = 4


def _cparams(n_axes):
    return pltpu.CompilerParams(dimension_semantics=("arbitrary",) * n_axes,
                                vmem_limit_bytes=VMEM_LIMIT)


def _dot(a, b):
    return jnp.dot(a, b, preferred_element_type=F32)


def _dot_nt(a, b):
    return lax.dot_general(a, b, (((1,), (1,)), ((), ())), preferred_element_type=F32)


def _split2(x):
    hi = x.astype(BF16)
    lo = (x - hi.astype(F32)).astype(BF16)
    return hi, lo


def _split3(x):
    hi = x.astype(BF16)
    r = x - hi.astype(F32)
    mid = r.astype(BF16)
    lo = (r - mid.astype(F32)).astype(BF16)
    return hi, mid, lo


def _dot_ref(a, b, nt=False):
    d = _dot_nt if nt else _dot
    return d(a.astype(BF16), b.astype(BF16))


def _dot_exact_lhs(a_bf16, b):
    bh, bm, bl = _split3(b)
    return _dot(a_bf16, bh) + (_dot(a_bf16, bm) + _dot(a_bf16, bl))


def _seq_onehot(row0, rows, grp):
    n_prompt_rows, tp_shift, ts_shift, n_prompt = grp
    r = row0 + lax.broadcasted_iota(I32, (rows, SEQ_TABLE_ROWS), 0)
    lane = lax.broadcasted_iota(I32, (rows, SEQ_TABLE_ROWS), 1)
    sid = jnp.where(r < n_prompt_rows, r >> tp_shift, n_prompt + ((r - n_prompt_rows) >> ts_shift))
    return jnp.where(lane == sid, 1.0, 0.0).astype(BF16)


def _head_segsum(x, e_ref):
    e = e_ref[...]
    outs = []
    for c in range(x.shape[1] // LANES):
        hi, lo = _split2(x[:, c * LANES:(c + 1) * LANES])
        outs.append(_dot(hi, e) + _dot(lo, e))
    return jnp.concatenate(outs, axis=1)


def _sigmoid(x):
    return 1.0 / (1.0 + jnp.exp(-x))


def _ada_kernel(c_ref, w_ref, b_ref, o_ref):
    o_ref[0] = _dot_ref(c_ref[...], w_ref[0]) + b_ref[0]


def _ada(c_pad, w_ada, b_ada):
    depth, d, n = w_ada.shape
    tn = 512
    return pl.pallas_call(
        _ada_kernel,
        out_shape=jax.ShapeDtypeStruct((depth, SEQ_TABLE_ROWS, n), F32),
        grid=(depth, n // tn),
        in_specs=[pl.BlockSpec((SEQ_TABLE_ROWS, d), lambda l, j: (0, 0)),
                  pl.BlockSpec((1, d, tn), lambda l, j: (l, 0, j)),
                  pl.BlockSpec((1, 1, tn), lambda l, j: (l, 0, j))],
        out_specs=pl.BlockSpec((1, SEQ_TABLE_ROWS, tn), lambda l, j: (l, 0, j)),
        compiler_params=_cparams(2), name="ada_mod",
    )(c_pad, w_ada, b_ada.reshape(depth, 1, n))


def _rms_modulated(x_ref, g_ref, sc_ref, sh_ref, grp):
    rows = x_ref.shape[0]
    x = x_ref[...]
    y = x * lax.rsqrt(jnp.mean(x * x, axis=-1, keepdims=True) + RMS_EPS) * g_ref[...]
    oh = _seq_onehot(pl.program_id(0) * rows, rows, grp)
    sc = _dot_exact_lhs(oh, sc_ref[...])
    sh = _dot_exact_lhs(oh, sh_ref[...])
    return y * (1.0 + sc) + sh


def _rms_mod_kernel(x_ref, g_ref, sc_ref, sh_ref, o_ref, *, grp):
    o_ref[...] = _rms_modulated(x_ref, g_ref, sc_ref, sh_ref, grp).astype(o_ref.dtype)


def _mod_spec(k):
    return pl.BlockSpec((SEQ_TABLE_ROWS, D_MODEL), lambda i, k=k: (0, k))


def _rms_mod(x, g, mod, k_sc, k_sh, grp):
    t = x.shape[0]
    return pl.pallas_call(
        functools.partial(_rms_mod_kernel, grp=grp),
        out_shape=jax.ShapeDtypeStruct((t, D_MODEL), BF16),
        grid=(t // ROW_TILE,),
        in_specs=[pl.BlockSpec((ROW_TILE, D_MODEL), lambda i: (i, 0)),
                  pl.BlockSpec((1, D_MODEL), lambda i: (0, 0)),
                  _mod_spec(k_sc), _mod_spec(k_sh)],
        out_specs=pl.BlockSpec((ROW_TILE, D_MODEL), lambda i: (i, 0)),
        compiler_params=_cparams(1), name="rms_mod",
    )(x, g.reshape(1, D_MODEL), mod, mod)


def _rms_router_kernel(x_ref, g_ref, sc_ref, sh_ref, wr_ref, br_ref, h_ref, route_ref, cnt_ref,
                       carry, *, grp):
    i = pl.program_id(0)
    rows = x_ref.shape[0]
    h = _rms_modulated(x_ref, g_ref, sc_ref, sh_ref, grp)
    h_ref[...] = h
    logits = _dot_ref(h, wr_ref[...]) + br_ref[...]
    lane = lax.broadcasted_iota(I32, (rows, LANES), 1)
    vals, idxs = [], []
    multi = jnp.zeros((rows, LANES), F32)
    for _ in range(TOP_K):
        m = jnp.max(logits, axis=1, keepdims=True)
        sel = jnp.min(jnp.where(logits == m, lane, LANES), axis=1, keepdims=True)
        hit = lane == sel
        vals.append(m)
        idxs.append(sel)
        multi = jnp.where(hit, 1.0, multi)
        logits = jnp.where(hit, NEG_BIG, logits)
    es = [jnp.exp(v - vals[0]) for v in vals]
    denom = es[0] + es[1] + es[2] + es[3]

    @pl.when(i == 0)
    def _():
        carry[...] = jnp.zeros_like(carry)

    r_i = lax.broadcasted_iota(I32, (rows, rows), 0)
    c_i = lax.broadcasted_iota(I32, (rows, rows), 1)
    tri = jnp.where(c_i < r_i, 1.0, 0.0).astype(BF16)
    cum = _dot(tri, multi.astype(BF16)) + carry[0:1, :]
    route = jnp.zeros((rows, LANES), F32)
    for j in range(TOP_K):
        pos = jnp.sum(jnp.where(lane == idxs[j], cum, 0.0), axis=1, keepdims=True)
        route = jnp.where(lane == j, idxs[j].astype(F32), route)
        route = jnp.where(lane == TOP_K + j, es[j] / denom, route)
        route = jnp.where(lane == 2 * TOP_K + j, pos, route)
    route_ref[...] = route
    new_cnt = carry[0:1, :] + jnp.sum(multi, axis=0, keepdims=True)
    carry[...] = jnp.broadcast_to(new_cnt, carry.shape)
    cnt_ref[...] = jnp.broadcast_to(new_cnt, cnt_ref.shape)


def _rms_router(x, g, mod, k_sc, k_sh, wr_pad, br_pad, grp):
    t = x.shape[0]
    return pl.pallas_call(
        functools.partial(_rms_router_kernel, grp=grp),
        out_shape=(jax.ShapeDtypeStruct((t, D_MODEL), F32),
                   jax.ShapeDtypeStruct((t, LANES), F32),
                   jax.ShapeDtypeStruct((SUBLANES, LANES), F32)),
        grid=(t // ROW_TILE,),
        in_specs=[pl.BlockSpec((ROW_TILE, D_MODEL), lambda i: (i, 0)),
                  pl.BlockSpec((1, D_MODEL), lambda i: (0, 0)),
                  _mod_spec(k_sc), _mod_spec(k_sh),
                  pl.BlockSpec((D_MODEL, LANES), lambda i: (0, 0)),
                  pl.BlockSpec((1, LANES), lambda i: (0, 0))],
        out_specs=(pl.BlockSpec((ROW_TILE, D_MODEL), lambda i: (i, 0)),
                   pl.BlockSpec((ROW_TILE, LANES), lambda i: (i, 0)),
                   pl.BlockSpec((SUBLANES, LANES), lambda i: (0, 0))),
        scratch_shapes=[pltpu.VMEM((SUBLANES, LANES), F32)],
        compiler_params=_cparams(1), name="rms_router",
    )(x, g.reshape(1, D_MODEL), mod, mod, wr_pad, br_pad)


def _final_norm_kernel(x_ref, g_ref, o_ref):
    x = x_ref[...]
    o_ref[...] = x * lax.rsqrt(jnp.mean(x * x, axis=-1, keepdims=True) + RMS_EPS) * g_ref[...]


def _final_norm(x, g):
    t = x.shape[0]
    return pl.pallas_call(
        _final_norm_kernel,
        out_shape=jax.ShapeDtypeStruct((t, D_MODEL), F32),
        grid=(t // ROW_TILE,),
        in_specs=[pl.BlockSpec((ROW_TILE, D_MODEL), lambda i: (i, 0)),
                  pl.BlockSpec((1, D_MODEL), lambda i: (0, 0))],
        out_specs=pl.BlockSpec((ROW_TILE, D_MODEL), lambda i: (i, 0)),
        compiler_params=_cparams(1), name="final_norm",
    )(x, g.reshape(1, D_MODEL))


def _mm_kernel(a_ref, w_ref, o_ref):
    o_ref[...] = _dot(a_ref[...], w_ref[...])


def _matmul(a, w, tm, tn):
    m, k = a.shape
    n = w.shape[1]
    return pl.pallas_call(
        _mm_kernel,
        out_shape=jax.ShapeDtypeStruct((m, n), F32),
        grid=(n // tn, m // tm),
        in_specs=[pl.BlockSpec((tm, k), lambda j, i: (i, 0)),
                  pl.BlockSpec((k, tn), lambda j, i: (0, j))],
        out_specs=pl.BlockSpec((tm, tn), lambda j, i: (i, j)),
        compiler_params=_cparams(2), name="proj_matmul",
    )(a, w)


def _mix_out_kernel(att_ref, rw_ref, wa_ref, wb_ref, x_ref, gt_ref, o_ref, *, grp):
    rows = x_ref.shape[0]
    mix = _dot(att_ref[...], wa_ref[...]) + _dot(rw_ref[...], wb_ref[...])
    oh = _seq_onehot(pl.program_id(1) * rows, rows, grp)
    o_ref[...] = x_ref[...] + _dot_exact_lhs(oh, gt_ref[...]) * mix


def _mix_out(att, rw, w_out, x, mod, k_gt, grp):
    t = x.shape[0]
    tm, tn = ROW_TILE, 1024
    nb = D_MODEL // tn
    return pl.pallas_call(
        functools.partial(_mix_out_kernel, grp=grp),
        out_shape=jax.ShapeDtypeStruct((t, D_MODEL), F32),
        grid=(nb, t // tm),
        in_specs=[pl.BlockSpec((tm, C_HEADS), lambda j, i: (i, 0)),
                  pl.BlockSpec((tm, C_HEADS), lambda j, i: (i, 0)),
                  pl.BlockSpec((C_HEADS, tn), lambda j, i: (0, j)),
                  pl.BlockSpec((C_HEADS, tn), lambda j, i: (1, j)),
                  pl.BlockSpec((tm, tn), lambda j, i: (i, j)),
                  pl.BlockSpec((SEQ_TABLE_ROWS, tn), lambda j, i, k=k_gt, nb=nb: (0, k * nb + j))],
        out_specs=pl.BlockSpec((tm, tn), lambda j, i: (i, j)),
        compiler_params=_cparams(2), name="mix_out",
    )(att, rw, w_out, w_out, x, mod)


def _pattn_kernel(q_ref, k_ref, v_ref, o_ref, kmean):
    blk = q_ref.shape[0]
    n_blk = k_ref.shape[0] // blk
    n_pair = q_ref.shape[1] // LANES
    n_pad = kmean.shape[1]
    qi = pl.program_id(2)
    lane = lax.broadcasted_iota(I32, (blk, LANES), 1)
    low = lane < HEAD_DIM
    heads = [(pr, e) for pr in range(n_pair) for e in range(2)]

    @pl.when(qi == 0)
    def _():
        for pr in range(n_pair):
            rows = [jnp.sum(k_ref[n * blk:(n + 1) * blk, pr * LANES:(pr + 1) * LANES], axis=0, keepdims=True)
                    * (1.0 / blk) for n in range(n_blk)]
            if n_pad > n_blk:
                rows.append(jnp.zeros((n_pad - n_blk, LANES), F32))
            kmean[pr] = jnp.concatenate(rows, axis=0)

    blk_i = lax.broadcasted_iota(I32, (n_pad, blk), 0)
    past = jnp.where(blk_i < qi, 1.0, 0.0)
    qh, selm = [], []
    for pr, e in heads:
        q = q_ref[:, pr * LANES:(pr + 1) * LANES] * (HEAD_DIM ** -0.5)
        qm = jnp.where(low, q, 0.0) if e == 0 else jnp.where(low, 0.0, q)
        qh.append(qm.astype(BF16))
        s = _dot_nt(kmean[pr].astype(BF16), qh[-1])
        cnt = jnp.zeros((n_pad, blk), F32)
        for m in range(n_blk):
            sm = s[m:m + 1, :]
            beats = (sm > s) | ((sm == s) & (m < blk_i))
            cnt = cnt + jnp.where(beats, 1.0, 0.0) * jnp.where(m < qi, 1.0, 0.0)
        sel_t = jnp.where(cnt < MOBA_TOPK, past, 0.0)
        sel_t = jnp.concatenate([sel_t, jnp.zeros((LANES - n_pad, blk), F32)], axis=0)
        selm.append(jnp.transpose(sel_t))
    row_i = lax.broadcasted_iota(I32, (blk, blk), 0)
    col_i = lax.broadcasted_iota(I32, (blk, blk), 1)
    causal = jnp.where(col_i <= row_i, 1.0, 0.0)

    def body(n, carry):
        off = pl.multiple_of(n * blk, blk)
        own = jnp.where(n == qi, 1.0, 0.0)
        out = []
        for h, (pr, e) in enumerate(heads):
            kb = k_ref[pl.ds(off, blk), pr * LANES:(pr + 1) * LANES].astype(BF16)
            vb = v_ref[pl.ds(off, blk), pr * LANES:(pr + 1) * LANES].astype(BF16)
            m_old, l_old, acc = carry[3 * h:3 * h + 3]
            logits = _dot_nt(qh[h], kb)
            flag = jnp.sum(jnp.where(lane == n, selm[h], 0.0), axis=1, keepdims=True)
            allowed = (own * causal + (1.0 - own) * flag) > 0.5
            masked = jnp.where(allowed, logits, NEG_BIG)
            m_new = jnp.maximum(m_old, jnp.max(masked, axis=1, keepdims=True))
            alpha = jnp.exp(m_old - m_new)
            p = jnp.where(allowed, jnp.exp(masked - m_new), 0.0)
            l_new = l_old * alpha + jnp.sum(p, axis=1, keepdims=True)
            acc = acc * alpha + _dot(p.astype(BF16), vb)
            out += [m_new, l_new, acc]
        return tuple(out)

    init = (jnp.full((blk, 1), NEG_BIG, F32), jnp.zeros((blk, 1), F32),
            jnp.zeros((blk, LANES), F32)) * len(heads)
    res = lax.fori_loop(0, qi + 1, body, init)
    for pr in range(n_pair):
        a, b = 6 * pr, 6 * pr + 3
        o_ref[:, pr * LANES:(pr + 1) * LANES] = jnp.where(
            low, res[a + 2] / res[a + 1], res[b + 2] / res[b + 1]).astype(o_ref.dtype)


def _prompt_attention(qkv, n_seq, seq_len):
    blk = MOBA_BLOCK
    nq = seq_len // blk
    npair = PROMPT_PAIRS_PER_STEP
    w = npair * LANES
    ng = N_PAIRS // npair
    n_pad = -(-nq // SUBLANES) * SUBLANES
    assert n_pad <= LANES
    return pl.pallas_call(
        _pattn_kernel,
        out_shape=jax.ShapeDtypeStruct((n_seq * seq_len, C_HEADS), BF16),
        grid=(n_seq, ng, nq),
        in_specs=[pl.BlockSpec((blk, w), lambda b, p, i: (b * nq + i, p)),
                  pl.BlockSpec((seq_len, w), lambda b, p, i: (b, ng + p)),
                  pl.BlockSpec((seq_len, w), lambda b, p, i: (b, 2 * ng + p))],
        out_specs=pl.BlockSpec((blk, w), lambda b, p, i: (b * nq + i, p)),
        scratch_shapes=[pltpu.VMEM((npair, n_pad, LANES), F32)],
        compiler_params=_cparams(3), name="moba_prompt",
    )(qkv, qkv, qkv)


def _sattn_k_kernel(pt_ref, qr_ref, *refs, n_grp):
    kp_refs, (lg_ref, sc_ref) = refs[:n_grp], refs[n_grp:]
    j = pl.program_id(1)
    qr = qr_ref[0]
    qb = qr.astype(BF16)
    pages_per_block = MOBA_BLOCK // PAGE_SIZE
    blocks_per_step = n_grp // pages_per_block
    lane = lax.broadcasted_iota(I32, sc_ref.shape[1:], 1)

    @pl.when(j == 0)
    def _():
        sc_ref[0] = jnp.zeros(sc_ref.shape[1:], F32)

    kts = [ref[0] for ref in kp_refs]
    for u in range(n_grp):
        lg_ref[0, u] = _dot(qb, kts[u].astype(BF16))
    sc = sc_ref[0]
    for n in range(blocks_per_step):
        ksum = kts[n * pages_per_block]
        for u in range(1, pages_per_block):
            ksum = ksum + kts[n * pages_per_block + u]
        kmean = jnp.sum(ksum, axis=1, keepdims=True) * (1.0 / MOBA_BLOCK)
        s = _dot(qb, jnp.broadcast_to(kmean, (C_HEADS, LANES)).astype(BF16))
        sc = jnp.where(lane == j * blocks_per_step + n, s, sc)
    sc_ref[0] = sc


def _sattn_v_kernel(pt_ref, lg_ref, sc_ref, *refs, n_new, n_grp, n_blocks):
    vp_refs = refs[:n_grp]
    qr_ref, kn_ref, vn_ref, o_ref, selm, m_s, l_s, acc = refs[n_grp:]
    j = pl.program_id(1)
    nr = lg_ref.shape[2]
    lane = lax.broadcasted_iota(I32, (nr, LANES), 1)
    pages_per_block = MOBA_BLOCK // PAGE_SIZE

    @pl.when(j == 0)
    def _():
        s = sc_ref[0]
        cnt = jnp.zeros((nr, LANES), F32)
        for m in range(n_blocks):
            sm = s[:, m:m + 1]
            cnt = cnt + jnp.where((sm > s) | ((sm == s) & (m < lane)), 1.0, 0.0)
        selm[...] = jnp.where((cnt < MOBA_TOPK) & (lane < n_blocks), 1.0, 0.0)
        m_s[...] = jnp.full(m_s.shape, NEG_BIG, F32)
        l_s[...] = jnp.zeros(l_s.shape, F32)
        acc[...] = jnp.zeros(acc.shape, F32)

    def accumulate(logits, allowed, pv_fns):
        masked = [jnp.where(a, x, NEG_BIG) for x, a in zip(logits, allowed)]
        m_old = m_s[...]
        m_new = m_old
        for x in masked:
            m_new = jnp.maximum(m_new, jnp.max(x, axis=1, keepdims=True))
        alpha = jnp.exp(m_old - m_new)
        l_new = l_s[...] * alpha
        pv = acc[...] * alpha
        for x, a, fn in zip(masked, allowed, pv_fns):
            p = jnp.where(a, jnp.exp(x - m_new), 0.0)
            l_new = l_new + jnp.sum(p, axis=1, keepdims=True)
            pv = pv + fn(p.astype(BF16))
        l_s[...] = l_new
        acc[...] = pv
        m_s[...] = m_new

    sel = selm[...]
    flags = [jnp.sum(jnp.where(lane == (j * n_grp + u) // pages_per_block, sel, 0.0), axis=1, keepdims=True)
             for u in range(n_grp)]
    accumulate([lg_ref[0, u] for u in range(n_grp)],
               [jnp.broadcast_to(f, (nr, LANES)) > 0.0 for f in flags],
               [functools.partial(lambda p, ref: _dot_nt(p, ref[0].astype(BF16)), ref=ref) for ref in vp_refs])

    @pl.when(j == pl.num_programs(1) - 1)
    def _():
        pad = jnp.zeros((LANES - n_new, C_HEADS), F32)
        kn = jnp.concatenate([kn_ref[...], pad], axis=0).astype(BF16)
        vn = jnp.concatenate([vn_ref[...], pad], axis=0).astype(BF16)
        logits = _dot_nt(qr_ref[0].astype(BF16), kn)
        row = lax.broadcasted_iota(I32, (nr, LANES), 0)
        accumulate([logits], [(lane <= (row & (n_new - 1))) & (lane < n_new)], [lambda p: _dot(p, vn)])
        out = acc[...] / l_s[...]
        r2 = lax.broadcasted_iota(I32, (nr, C_HEADS), 0)
        c2 = lax.broadcasted_iota(I32, (nr, C_HEADS), 1)
        own = jnp.where((r2 >> _log2(n_new)) == (c2 >> _log2(HEAD_DIM)), out, 0.0)
        o_ref[...] = jnp.sum(own.reshape(N_HEADS, n_new, C_HEADS), axis=0)


def _sample_attention(qkv, row0, n_seq, n_new, cache_kt, cache_vt, page_table, page_base):
    n_pages = page_table.shape[1]
    n_grp = SAMPLE_PAGES_PER_STEP
    n_blocks = (n_pages * PAGE_SIZE) // MOBA_BLOCK
    assert n_pages % n_grp == 0 and n_grp % (MOBA_BLOCK // PAGE_SIZE) == 0 and n_blocks <= LANES
    n_steps = n_pages // n_grp
    nr = N_HEADS * n_new
    q = qkv[row0:row0 + n_seq * n_new, :C_HEADS].reshape(n_seq, 1, n_new, C_HEADS) * (HEAD_DIM ** -0.5)
    head_of_col = (jnp.arange(C_HEADS, dtype=I32) // HEAD_DIM)[None, None, None, :]
    head_of_row = jnp.arange(N_HEADS, dtype=I32)[None, :, None, None]
    qrows = jnp.where(head_of_row == head_of_col, q, 0.0).reshape(n_seq, nr, C_HEADS)
    pt = (page_table + page_base).astype(I32)
    page_specs = [pl.BlockSpec((1, C_HEADS, PAGE_SIZE), lambda b, j, pt, u=u: (pt[b, j * n_grp + u], 0, 0))
                  for u in range(n_grp)]
    lg_spec = pl.BlockSpec((1, n_grp, nr, PAGE_SIZE), lambda b, j, pt: (b, j, 0, 0))
    sc_spec = pl.BlockSpec((1, nr, LANES), lambda b, j, pt: (b, 0, 0))
    qr_spec = pl.BlockSpec((1, nr, C_HEADS), lambda b, j, pt: (b, 0, 0))

    logits, scores = pl.pallas_call(
        functools.partial(_sattn_k_kernel, n_grp=n_grp),
        out_shape=(jax.ShapeDtypeStruct((n_seq, n_pages, nr, PAGE_SIZE), F32),
                   jax.ShapeDtypeStruct((n_seq, nr, LANES), F32)),
        grid_spec=pltpu.PrefetchScalarGridSpec(
            num_scalar_prefetch=1, grid=(n_seq, n_steps),
            in_specs=[qr_spec] + page_specs,
            out_specs=(lg_spec, sc_spec)),
        compiler_params=_cparams(2), name="moba_sample_k",
    )(pt, qrows, *([cache_kt] * n_grp))

    rb = row0 // n_new
    return pl.pallas_call(
        functools.partial(_sattn_v_kernel, n_new=n_new, n_grp=n_grp, n_blocks=n_blocks),
        out_shape=jax.ShapeDtypeStruct((n_seq * n_new, C_HEADS), F32),
        grid_spec=pltpu.PrefetchScalarGridSpec(
            num_scalar_prefetch=1, grid=(n_seq, n_steps),
            in_specs=[lg_spec, sc_spec] + page_specs + [
                qr_spec,
                pl.BlockSpec((n_new, C_HEADS), lambda b, j, pt: (rb + b, 1)),
                pl.BlockSpec((n_new, C_HEADS), lambda b, j, pt: (rb + b, 2))],
            out_specs=pl.BlockSpec((n_new, C_HEADS), lambda b, j, pt: (b, 0)),
            scratch_shapes=[pltpu.VMEM((nr, LANES), F32), pltpu.VMEM((nr, 1), F32),
                            pltpu.VMEM((nr, 1), F32), pltpu.VMEM((nr, C_HEADS), F32)]),
        compiler_params=_cparams(2), name="moba_sample_v",
    )(pt, logits, scores, *([cache_vt] * n_grp), qrows, qkv, qkv)


def _rwkv_prep_kernel(*refs, has_vres, grp):
    if has_vres:
        (pr_ref, edge_ref, s0_ref, mu_ref, w0_ref, a0_ref, wup_ref, aup_ref, gup_ref, kk_ref_, ka_ref,
         e_ref, vf_ref, v0_ref, vdn_ref, vup_ref,
         r_o, w_o, lw_o, k_o, v_o, kk_o, b_o, g_o) = refs
    else:
        (pr_ref, edge_ref, s0_ref, mu_ref, w0_ref, a0_ref, wup_ref, aup_ref, gup_ref, kk_ref_, ka_ref,
         e_ref, r_o, w_o, lw_o, k_o, v_o, kk_o, b_o, g_o) = refs
    n_prompt_rows, tp_shift, ts_shift, _ = grp
    pr = pr_ref[...]
    tm = pr.shape[0]
    loc = lax.broadcasted_iota(I32, (tm, 1), 0)
    glob = pl.program_id(0) * tm + loc
    prev = jnp.where(loc == 0, edge_ref[SUBLANES - 1:SUBLANES, :], pltpu.roll(pr, 1, 0))
    first_p = (glob < n_prompt_rows) & ((glob & ((1 << tp_shift) - 1)) == 0)
    first_s = (glob >= n_prompt_rows) & (((glob - n_prompt_rows) & ((1 << ts_shift) - 1)) == 0)
    prev = jnp.where(first_p, 0.0, jnp.where(first_s, s0_ref[...], prev))
    xm = pr + (prev - pr) * mu_ref[...]
    c = C_HEADS
    r, k, v, lora = xm[:, :c], xm[:, c:2 * c], xm[:, 2 * c:3 * c], xm[:, 3 * c:]
    z = w0_ref[...] + _dot_ref(jnp.tanh(lora), wup_ref[...])
    w_log = -(jnp.maximum(-z, 0.0) + jnp.log(1.0 + jnp.exp(-jnp.abs(z)))) - 0.5
    log_decay = -jnp.exp(w_log)
    decay = jnp.exp(log_decay)
    a = _sigmoid(a0_ref[...] + _dot_ref(lora, aup_ref[...]))
    g = _dot_ref(_sigmoid(lora), gup_ref[...])
    if has_vres:
        gate = _sigmoid(v0_ref[...] + _dot_ref(_dot_ref(v, vdn_ref[...]), vup_ref[...]))
        v = v + (vf_ref[...] - v) * gate
    kk = k * kk_ref_[...]
    ss = _head_segsum(kk * kk, e_ref)
    kk = kk * lax.rsqrt(jnp.maximum(ss, 1e-24))
    r_o[...] = r
    w_o[...] = decay
    lw_o[...] = log_decay
    k_o[...] = k * (1.0 + (a - 1.0) * ka_ref[...])
    v_o[...] = v
    kk_o[...] = kk
    b_o[...] = kk * a
    g_o[...] = g


def _rwkv_prep(pr, shift_rows, vecs, mats, e128, vres, grp):
    t = pr.shape[0]
    tm = 128
    c = C_HEADS
    assert grp[0] % tm == 0 and (t - grp[0]) % tm == 0
    n_prompt_tiles = grp[0] // tm
    row = lambda w: pl.BlockSpec((tm, w), lambda i: (i, 0))
    vec = lambda w: pl.BlockSpec((1, w), lambda i: (0, 0))
    full = lambda a: pl.BlockSpec(a.shape, lambda i: (0, 0))
    edge = pl.BlockSpec((SUBLANES, C_SHIFT_PAD), lambda i: (jnp.maximum(i * (tm // SUBLANES) - 1, 0), 0))
    s0 = pl.BlockSpec((tm, C_SHIFT_PAD), lambda i: (jnp.maximum(i - n_prompt_tiles, 0), 0))
    mu, w0, a0, k_k, k_a = vecs
    wup, aup, gup = mats
    args = [pr, pr, shift_rows, mu, w0, a0, wup, aup, gup, k_k, k_a, e128]
    specs = [row(C_SHIFT_PAD), edge, s0, vec(C_SHIFT_PAD), vec(c), vec(c),
             full(wup), full(aup), full(gup), vec(c), vec(c), full(e128)]
    if vres is not None:
        v_first, v0, vdn, vup = vres
        args += [v_first, v0, vdn, vup]
        specs += [row(c), vec(c), full(vdn), full(vup)]
    return pl.pallas_call(
        functools.partial(_rwkv_prep_kernel, has_vres=vres is not None, grp=grp),
        out_shape=tuple(jax.ShapeDtypeStruct((t, c), F32) for _ in range(8)),
        grid=(t // tm,),
        in_specs=specs,
        out_specs=tuple(row(c) for _ in range(8)),
        compiler_params=_cparams(1), name="rwkv_prep",
    )(*args)


def _rwkv_rec_kernel(r_ref, w_ref, k_ref, v_ref, kk_ref, b_ref, s0_ref, y_ref, st_ref, state):
    nb, tc = r_ref.shape[:2]
    c = pl.program_id(1)

    @pl.when(c == 0)
    def _():
        state[...] = s0_ref[...]

    lane = lax.broadcasted_iota(I32, (HEAD_DIM, LANES), 1)
    row = lax.broadcasted_iota(I32, (HEAD_DIM, LANES), 0)
    low = lane < HEAD_DIM
    eye = jnp.where(low, lane, lane - HEAD_DIM) == row

    def seg(x):
        e = jnp.sum(jnp.where(low, x, 0.0), axis=1, keepdims=True)
        o = jnp.sum(jnp.where(low, 0.0, x), axis=1, keepdims=True)
        return jnp.where(low, e, o)

    def step(t, carry):
        for bi in range(nb):
            r_t, w_t, k_t, v_t, kk_t, b_t = (ref[bi, t] for ref in
                                             (r_ref, w_ref, k_ref, v_ref, kk_ref, b_ref))
            y_rows = []
            for p in range(N_PAIRS):
                s_old = state[bi, p]
                sa = seg(s_old * (-kk_t[p:p + 1, :]))
                v_col = seg(jnp.where(eye, v_t[p:p + 1, :], 0.0))
                s_new = s_old * w_t[p:p + 1, :] + sa * b_t[p:p + 1, :] + v_col * k_t[p:p + 1, :]
                state[bi, p] = s_new
                y_b = seg(s_new * r_t[p:p + 1, :])
                y_rows.append(jnp.sum(jnp.where(eye, y_b, 0.0), axis=0, keepdims=True))
            y_ref[bi, t] = jnp.concatenate(y_rows, axis=0)
        return carry

    lax.fori_loop(0, tc, step, 0)

    @pl.when(c == pl.num_programs(1) - 1)
    def _():
        st_ref[...] = state[...]


def _rwkv_rec(seqs, s0, nb, tc):
    b, t = seqs[0].shape[:2]
    blk = pl.BlockSpec((nb, tc, N_PAIRS, LANES), lambda g, i: (g, i, 0, 0))
    st = pl.BlockSpec((nb, N_PAIRS, HEAD_DIM, LANES), lambda g, i: (g, 0, 0, 0))
    return pl.pallas_call(
        _rwkv_rec_kernel,
        out_shape=(jax.ShapeDtypeStruct((b, t, N_PAIRS, LANES), F32),
                   jax.ShapeDtypeStruct((b, N_PAIRS, HEAD_DIM, LANES), F32)),
        grid=(b // nb, t // tc),
        in_specs=[blk] * 6 + [st],
        out_specs=(blk, st),
        scratch_shapes=[pltpu.VMEM((nb, N_PAIRS, HEAD_DIM, LANES), F32)],
        compiler_params=_cparams(2), name="rwkv_rec",
    )(*seqs, s0)


def _sp(x):
    return _split2(x)


def _mm3(a, b, mode="nn"):
    if mode == "nt":
        d = _dot_nt
    elif mode == "tn":
        d = lambda x, y: lax.dot_general(x, y, (((0,), (0,)), ((), ())), preferred_element_type=F32)
    else:
        d = _dot
    return d(a[0], b[0]) + (d(a[0], b[1]) + d(a[1], b[0]))


def _rwkv_chunk_kernel(r_ref, lw_ref, k_ref, v_ref, kk_ref, b_ref, y_ref, st_ref, hbd, *, pairs):
    c = pl.program_id(2)
    n_c = r_ref.shape[0]
    n2 = 2 * n_c
    log_c = _log2(n_c)

    @pl.when(c == 0)
    def _():
        hbd[...] = jnp.zeros(hbd.shape, F32)

    lane = lax.broadcasted_iota(I32, (n_c, LANES), 1)
    low = lane < HEAD_DIM
    i = lax.broadcasted_iota(I32, (n2, n2), 0)
    j = lax.broadcasted_iota(I32, (n2, n2), 1)
    same = (i >> log_c) == (j >> log_c)
    ti = i & (n_c - 1)
    tj = j & (n_c - 1)
    m_stril = jnp.where(same & (tj < ti), 1.0, 0.0)
    m_tril = jnp.where(same & (tj <= ti), 1.0, 0.0)
    eye = jnp.where(i == j, 1.0, 0.0)
    m_blk = {s: jnp.where((i >> s) == (j >> s), 1.0, 0.0) for s in range(3, log_c + 1)}
    ci = lax.broadcasted_iota(I32, (n_c, n_c), 0)
    cj = lax.broadcasted_iota(I32, (n_c, n_c), 1)
    tri_c = jnp.where(cj <= ci, 1.0, 0.0).astype(BF16)

    def stack(x):
        return jnp.concatenate([jnp.where(low, x, 0.0), jnp.where(low, 0.0, x)], axis=0)

    def each(fn, *cols):
        return [fn(*args) for args in zip(*cols)]

    sls = [slice(q * LANES, (q + 1) * LANES) for q in range(pairs)]
    lw = [lw_ref[:, sl] for sl in sls]
    cum = each(lambda x: _dot_exact_lhs(tri_c, x), lw)
    cum_c = each(lambda x: x[n_c - 1:n_c, :], cum)
    e_neg = each(lambda x: jnp.exp(-x), cum)
    e_tail = each(lambda x, xc: jnp.exp(xc - x), cum, cum_c)
    kk = [kk_ref[:, sl] for sl in sls]
    b = [b_ref[:, sl] for sl in sls]
    k = [k_ref[:, sl] for sl in sls]
    a_s = each(lambda x, cu, l: _sp(stack(-x * jnp.exp(cu - l))), kk, cum, lw)
    b_s = each(lambda x, e: _sp(stack(x * e)), b, e_neg)
    k_s = each(lambda x, e: _sp(stack(x * e)), k, e_neg)
    r_st = [stack(r_ref[:, sl] * jnp.exp(cu)) for sl, cu in zip(sls, cum)]
    r_s = each(_sp, r_st)
    bh_s = each(lambda x, e: _sp(stack(x * e)), b, e_tail)
    kh_s = each(lambda x, e: _sp(stack(x * e)), k, e_tail)
    v_s = [_sp(stack(v_ref[:, sl])) for sl in sls]

    l_ab = each(lambda x, y: m_stril * _mm3(x, y, "nt"), a_s, b_s)
    l_ak = each(lambda x, y: _sp(m_stril * _mm3(x, y, "nt")), a_s, k_s)
    m_rb = each(lambda x, y: _sp(m_tril * _mm3(x, y, "nt")), r_s, b_s)
    m_rk = each(lambda x, y: _sp(m_tril * _mm3(x, y, "nt")), r_s, k_s)

    d1 = each(lambda x: x * m_blk[3], l_ab)
    d1s = each(_sp, d1)
    d2 = each(lambda x: _mm3(x, x), d1s)
    d2s = each(_sp, d2)
    d4 = each(lambda x: _mm3(x, x), d2s)
    i12 = each(lambda x, y: _sp(_mm3(_sp(eye + x), _sp(eye + y))), d1, d2)
    inv = each(lambda x, y: _mm3(x, _sp(eye + y)), i12, d4)
    for s in range(4, log_c + 1):
        l_m = each(lambda x: _sp(x * (m_blk[s] - m_blk[s - 1])), l_ab)
        inv_s = each(_sp, inv)
        t_m = each(lambda x, y: _sp(_mm3(x, y)), inv_s, l_m)
        inv = each(lambda x, y, z: x + _mm3(y, z), inv, t_m, inv_s)
    inv_s = each(_sp, inv)
    w1 = each(lambda x, y: _sp(_mm3(x, y)), inv_s, a_s)
    lv = each(lambda x, y: _sp(_mm3(x, y)), l_ak, v_s)
    u0 = each(lambda x, y: _sp(_mm3(x, y)), inv_s, lv)
    y1 = each(lambda x, y, z: x + _mm3(y, z), r_st, m_rb, w1)
    y0 = each(lambda x, y, z, u: _mm3(x, y) + _mm3(z, u), m_rb, u0, m_rk, v_s)
    g = each(lambda xc, x, y: _sp(eye * jnp.exp(xc) + _mm3(x, y, "tn")), cum_c, bh_s, w1)
    h_add = each(lambda x, y, z, u: _mm3(x, y, "tn") + _mm3(z, u, "tn"), bh_s, u0, kh_s, v_s)
    y1c = each(lambda x: _sp(x[:n_c] + x[n_c:]), y1)
    y0c = each(lambda x: x[:n_c] + x[n_c:], y0)

    h_s = [_sp(hbd[q]) for q in range(pairs)]
    for q in range(pairs):
        y_ref[:, sls[q]] = _mm3(y1c[q], h_s[q]) + y0c[q]
    for q in range(pairs):
        hbd[q] = _mm3(g[q], h_s[q]) + h_add[q]

    @pl.when(c == pl.num_programs(2) - 1)
    def _():
        st_ref[0] = hbd[...]


def _rwkv_chunked(seqs, n_seq, seq_len, chunk, pairs):
    n_chunks = seq_len // chunk
    n_grp = N_PAIRS // pairs
    blk = pl.BlockSpec((chunk, pairs * LANES), lambda b, p, c: (b * n_chunks + c, p))
    return pl.pallas_call(
        functools.partial(_rwkv_chunk_kernel, pairs=pairs),
        out_shape=(jax.ShapeDtypeStruct((n_seq * seq_len, C_HEADS), F32),
                   jax.ShapeDtypeStruct((n_seq * n_grp, pairs, LANES, LANES), F32)),
        grid=(n_seq, n_grp, n_chunks),
        in_specs=[blk] * 6,
        out_specs=(blk, pl.BlockSpec((1, pairs, LANES, LANES), lambda b, p, c: (b * n_grp + p, 0, 0, 0))),
        scratch_shapes=[pltpu.VMEM((pairs, LANES, LANES), F32)],
        compiler_params=_cparams(3), name="rwkv_chunked",
    )(*seqs)


def _unpack_hbd(h, n_seq):
    h = h.reshape(n_seq, N_PAIRS, 2, HEAD_DIM, 2, HEAD_DIM)
    diag = jnp.stack([h[:, :, 0, :, 0, :], h[:, :, 1, :, 1, :]], axis=2)
    return jnp.transpose(diag, (0, 1, 2, 4, 3)).reshape(n_seq, N_HEADS, HEAD_DIM, HEAD_DIM)


def _rwkv_post_kernel(y_ref, r_ref, k_ref, v_ref, g_ref, lnw_ref, lnb_ref, rk_ref, e_ref, o_ref):
    y = y_ref[...]
    inv = 1.0 / HEAD_DIM
    mean = _head_segsum(y, e_ref) * inv
    yc = y - mean
    var = _head_segsum(yc * yc, e_ref) * inv
    yn = yc * lax.rsqrt(var + LN_X_EPS) * lnw_ref[...] + lnb_ref[...]
    bonus = _head_segsum(r_ref[...] * k_ref[...] * rk_ref[...], e_ref) * v_ref[...]
    o_ref[...] = ((yn + bonus) * g_ref[...]).astype(o_ref.dtype)


def _rwkv_post(y, r, k, v, g, ln_w, ln_b, r_k, e128):
    t, c = y.shape
    row = pl.BlockSpec((ROW_TILE, c), lambda i: (i, 0))
    vec = pl.BlockSpec((1, c), lambda i: (0, 0))
    return pl.pallas_call(
        _rwkv_post_kernel,
        out_shape=jax.ShapeDtypeStruct((t, c), BF16),
        grid=(t // ROW_TILE,),
        in_specs=[row] * 5 + [vec] * 3 + [pl.BlockSpec(e128.shape, lambda i: (0, 0))],
        out_specs=row,
        compiler_params=_cparams(1), name="rwkv_post",
    )(y, r, k, v, g, ln_w, ln_b, r_k, e128)


def _pack_state(s):
    b = s.shape[0]
    s = s.reshape(b, N_PAIRS, 2, HEAD_DIM, HEAD_DIM)
    return jnp.transpose(s, (0, 1, 3, 2, 4)).reshape(b, N_PAIRS, HEAD_DIM, LANES)


def _unpack_state(s):
    b = s.shape[0]
    s = s.reshape(b, N_PAIRS, HEAD_DIM, 2, HEAD_DIM)
    return jnp.transpose(s, (0, 1, 3, 2, 4)).reshape(b, N_HEADS, HEAD_DIM, HEAD_DIM)


def _gather_rows_kernel(idx_ref, nrows_ref, src_ref, o_ref, buf, sem):
    rows = o_ref.shape[0]
    base = pl.program_id(0) * rows

    @pl.when(base < nrows_ref[0])
    def _():
        def start(r, carry):
            pltpu.make_async_copy(src_ref.at[pl.ds(idx_ref[base + r], 1), :],
                                  buf.at[pl.ds(r, 1), :], sem).start()
            return carry

        lax.fori_loop(0, rows, start, 0, unroll=8)
        pltpu.make_async_copy(src_ref.at[pl.ds(0, rows), :], buf, sem).wait()
        o_ref[...] = buf[...].astype(o_ref.dtype)

    @pl.when(base >= nrows_ref[0])
    def _():
        o_ref[...] = jnp.zeros(o_ref.shape, o_ref.dtype)


def _gather_rows(src, idx, n_valid, rows_per_step):
    n = idx.shape[0]
    w = src.shape[1]
    assert n % rows_per_step == 0
    return pl.pallas_call(
        _gather_rows_kernel,
        out_shape=jax.ShapeDtypeStruct((n, w), BF16),
        grid_spec=pltpu.PrefetchScalarGridSpec(
            num_scalar_prefetch=2, grid=(n // rows_per_step,),
            in_specs=[pl.BlockSpec(memory_space=pl.ANY)],
            out_specs=pl.BlockSpec((rows_per_step, w), lambda i, idx, nv: (i, 0)),
            scratch_shapes=[pltpu.VMEM((rows_per_step, w), src.dtype), pltpu.SemaphoreType.DMA]),
        compiler_params=_cparams(1), name="gather_rows",
    )(idx, n_valid, src)


def _expert_kernel(te_ref, nu_ref, x_ref, w1_ref, b1_ref, w2_ref, b2_ref, sel_ref, o_ref):
    t = pl.program_id(0)
    j = pl.program_id(1)
    tm = o_ref.shape[0]

    @pl.when(t < nu_ref[0])
    def _():
        w1b = w1_ref[0].astype(BF16)
        w2b = w2_ref[0].astype(BF16)
        sub = tm // EXPERT_SUBTILES
        rows = [slice(s * sub, (s + 1) * sub) for s in range(EXPERT_SUBTILES)]
        hh = [_dot(x_ref[rs, :], w1b) + b1_ref[0] for rs in rows]
        width = hh[0].shape[1]
        nxt = [pltpu.roll(h, width - 1, 1) for h in hh]
        glu = [jnp.minimum(h, SWIGLU_LIMIT) for h in hh]
        act = [g * _sigmoid(SWIGLU_ALPHA * g) * (jnp.clip(n, -SWIGLU_LIMIT, SWIGLU_LIMIT) + 1.0)
               for g, n in zip(glu, nxt)]
        act = [_dot(a.astype(BF16), sel_ref[...]).astype(BF16) for a in act]
        part = [_dot(a, w2b) for a in act]

        @pl.when(j == 0)
        def _():
            for rs, p in zip(rows, part):
                o_ref[rs, :] = p + b2_ref[0]

        @pl.when(j > 0)
        def _():
            for rs, p in zip(rows, part):
                o_ref[rs, :] = o_ref[rs, :] + p

    @pl.when((t >= nu_ref[0]) & (j == 0))
    def _():
        o_ref[...] = jnp.zeros(o_ref.shape, F32)


def _expert_mlp(xs, tile_expert, n_used, w1, b1, w2, b2):
    n_rows = xs.shape[0]
    tm, bf = EXPERT_TILE, EXPERT_FBLOCK
    n_tiles = n_rows // tm
    nf = D_EXPERT // bf
    sel = (jnp.arange(2 * bf, dtype=I32)[:, None] == 2 * jnp.arange(bf, dtype=I32)[None, :]).astype(BF16)
    last = lambda t, nu: jnp.minimum(t, nu[0] - 1)
    return pl.pallas_call(
        _expert_kernel,
        out_shape=jax.ShapeDtypeStruct((n_rows, D_MODEL), F32),
        grid_spec=pltpu.PrefetchScalarGridSpec(
            num_scalar_prefetch=2, grid=(n_tiles, nf),
            in_specs=[pl.BlockSpec((tm, D_MODEL), lambda t, j, te, nu: (last(t, nu), 0)),
                      pl.BlockSpec((1, D_MODEL, 2 * bf), lambda t, j, te, nu: (te[t], 0, j)),
                      pl.BlockSpec((1, 1, 2 * bf), lambda t, j, te, nu: (te[t], 0, j)),
                      pl.BlockSpec((1, bf, D_MODEL), lambda t, j, te, nu: (te[t], j, 0)),
                      pl.BlockSpec((1, 1, D_MODEL), lambda t, j, te, nu: (te[t], 0, 0)),
                      pl.BlockSpec((2 * bf, bf), lambda t, j, te, nu: (0, 0))],
            out_specs=pl.BlockSpec((tm, D_MODEL), lambda t, j, te, nu: (t, 0))),
        compiler_params=_cparams(2), name="expert_mlp",
    )(tile_expert, n_used, xs, w1.reshape(-1, D_MODEL, 2 * D_EXPERT), b1.reshape(-1, 1, 2 * D_EXPERT),
      w2.reshape(-1, D_EXPERT, D_MODEL), b2.reshape(-1, 1, D_MODEL), sel)


def _combine_kernel(slot_ref, ys_ref, x_ref, route_ref, gt_ref, o_ref, ybuf, sem, *, grp):
    rows = x_ref.shape[0]
    base = pl.program_id(0) * rows

    def copy(r, j):
        return pltpu.make_async_copy(ys_ref.at[pl.ds(slot_ref[(base + r) * TOP_K + j], 1), :],
                                     ybuf.at[j, pl.ds(r, 1), :], sem)

    def start(r, carry):
        for j in range(TOP_K):
            copy(r, j).start(priority=j % 2)
        return carry

    def wait(r, carry):
        for j in range(TOP_K):
            copy(r, j).wait()
        return carry

    lax.fori_loop(0, rows, start, 0, unroll=4)
    lax.fori_loop(0, rows, wait, 0, unroll=4)
    route = route_ref[...]
    moe = jnp.zeros(x_ref.shape, F32)
    for j in range(TOP_K):
        moe = moe + route[:, TOP_K + j:TOP_K + j + 1] * ybuf[j]
    oh = _seq_onehot(base, rows, grp)
    o_ref[...] = x_ref[...] + _dot_exact_lhs(oh, gt_ref[...]) * moe


def _combine(x, ys, slots, route, mod, k_gt, grp):
    t = x.shape[0]
    tm = 128
    return pl.pallas_call(
        functools.partial(_combine_kernel, grp=grp),
        out_shape=jax.ShapeDtypeStruct((t, D_MODEL), F32),
        grid_spec=pltpu.PrefetchScalarGridSpec(
            num_scalar_prefetch=1, grid=(t // tm,),
            in_specs=[pl.BlockSpec(memory_space=pl.ANY),
                      pl.BlockSpec((tm, D_MODEL), lambda i, s: (i, 0)),
                      pl.BlockSpec((tm, LANES), lambda i, s: (i, 0)),
                      pl.BlockSpec((SEQ_TABLE_ROWS, D_MODEL), lambda i, s, k=k_gt: (0, k))],
            out_specs=pl.BlockSpec((tm, D_MODEL), lambda i, s: (i, 0)),
            scratch_shapes=[pltpu.VMEM((TOP_K, tm, D_MODEL), F32), pltpu.SemaphoreType.DMA]),
        compiler_params=_cparams(1), name="moe_combine",
    )(slots, ys, x, route, mod)


def _moe_layer(x, g, mod, wr_pad, br_pad, w1, b1, w2, b2, layer, grp):
    t = x.shape[0]
    h2, route, counts = _rms_router(x, g, mod, 4, 3, wr_pad, br_pad, grp)
    tm = EXPERT_TILE
    n_tiles = (t * TOP_K) // tm + N_EXPERTS
    n_tiles = -(-n_tiles // GATHER_TILES_PER_STEP) * GATHER_TILES_PER_STEP
    cnt = counts[0, :N_EXPERTS].astype(I32)
    tiles_per = (cnt + tm - 1) // tm
    tile_end = jnp.cumsum(tiles_per)
    group_start = (tile_end - tiles_per) * tm
    idx = route[:, :TOP_K].astype(I32)
    pos = route[:, 2 * TOP_K:3 * TOP_K].astype(I32)
    slots = (group_start[idx] + pos).reshape(-1)
    token_of_pair = jnp.repeat(jnp.arange(t, dtype=I32), TOP_K)
    src = jnp.zeros((n_tiles * tm,), I32).at[slots].set(token_of_pair)
    n_used = tile_end[-1:].astype(I32)
    tile_expert = jnp.searchsorted(tile_end, jnp.arange(n_tiles, dtype=I32), side="right").astype(I32)
    tile_expert = jnp.minimum(tile_expert, N_EXPERTS - 1)
    tile_expert = jnp.where(jnp.arange(n_tiles) < n_used[0], tile_expert,
                            tile_expert[jnp.maximum(n_used[0] - 1, 0)])
    xs = _gather_rows(h2, src, n_used * tm, GATHER_TILES_PER_STEP * tm)
    ys = _expert_mlp(xs, tile_expert + layer * N_EXPERTS, n_used, w1, b1, w2, b2)
    return _combine(x, ys, slots, route, mod, 5, grp)


def _log2(n):
    s = n.bit_length() - 1
    assert (1 << s) == n, "sequence lengths must be powers of two"
    return s


def kernel(x_prompt, x_sample, cache_k, cache_v, page_table, state_shift, state_wkv, c_prompt, c_sample,
           w_ada, b_ada, norm_mix, norm_ffn, w_in, w_out, rw_mu, rw_w0, rw_w_up, rw_a0, rw_a_up,
           rw_g_up, rw_k_k, rw_k_a, rw_r_k, rw_ln_w, rw_ln_b, rw_v0, rw_v_down, rw_v_up,
           w_router, b_router, w_mlp1, b_mlp1, w_mlp2, b_mlp2, norm_final):
    bp, tp, d = x_prompt.shape
    bs, ts, _ = x_sample.shape
    depth = w_ada.shape[0]
    n_pool = cache_k.shape[1]
    np_rows, ns_rows = bp * tp, bs * ts
    t_all = np_rows + ns_rows
    assert d == D_MODEL and t_all % ROW_TILE == 0 and np_rows % ROW_TILE == 0
    assert bp + bs <= SEQ_TABLE_ROWS and tp % MOBA_BLOCK == 0 and ts <= SUBLANES
    grp = (np_rows, _log2(tp), _log2(ts), bp)
    c = C_HEADS

    x = jnp.concatenate([x_prompt.reshape(np_rows, d), x_sample.reshape(ns_rows, d)], axis=0)
    c_pad = jnp.zeros((SEQ_TABLE_ROWS, d), F32).at[:bp + bs].set(jnp.concatenate([c_prompt, c_sample], axis=0))
    mod = _ada(c_pad, w_ada, b_ada)

    cache_kt = jnp.transpose(cache_k, (0, 1, 3, 4, 2)).reshape(depth * n_pool, c, PAGE_SIZE)
    cache_vt = jnp.transpose(cache_v, (0, 1, 3, 4, 2)).reshape(depth * n_pool, c, PAGE_SIZE)
    e128 = (jnp.arange(LANES)[:, None] // HEAD_DIM == jnp.arange(LANES)[None, :] // HEAD_DIM).astype(BF16)

    def lora_pad(w, start):
        return jnp.zeros((C_LORA_PAD, c), F32).at[start:start + w.shape[0]].set(w)

    k_out, v_out, shift_p, shift_s, wkv_p, wkv_s = [], [], [], [], [], []
    v_first = None
    for l in range(depth):
        h = _rms_mod(x, norm_mix[l], mod[l], 1, 0, grp)
        w_qkv = w_in[l][:, :3 * c].astype(BF16)
        w_pr = jnp.zeros((d, C_SHIFT_PAD), BF16).at[:, :C_SHIFT].set(w_in[l][:, 3 * c:].astype(BF16))
        qkv = _matmul(h, w_qkv, 768, 1024)
        pr = _matmul(h, w_pr, 768, 512)

        att_p = _prompt_attention(qkv, bp, tp)
        att_s = _sample_attention(qkv, np_rows, bs, ts, cache_kt, cache_vt, page_table, l * n_pool)
        att = jnp.concatenate([att_p, att_s.astype(BF16)], axis=0)

        shift_rows = jnp.zeros((bs, ts, C_SHIFT_PAD), F32).at[:, 0, :C_SHIFT].set(state_shift[l])
        shift_rows = shift_rows.reshape(ns_rows, C_SHIFT_PAD)
        mu =jnp.zeros((1, C_SHIFT_PAD), F32).at[0, :C_SHIFT].set(rw_mu[l])
        vecs = (mu, rw_w0[l].reshape(1, c), rw_a0[l].reshape(1, c), rw_k_k[l].reshape(1, c),
                rw_k_a[l].reshape(1, c))
        mats = (lora_pad(rw_w_up[l], 0), lora_pad(rw_a_up[l], D_DECAY_LORA),
                lora_pad(rw_g_up[l], D_DECAY_LORA + D_AAA_LORA))
        vres = None
        if l > 0:
            vdn = jnp.zeros((c, LANES), F32).at[:, :D_MV_LORA].set(rw_v_down[l - 1])
            vup = jnp.zeros((LANES, c), F32).at[:D_MV_LORA].set(rw_v_up[l - 1])
            vres = (v_first, rw_v0[l - 1].reshape(1, c), vdn, vup)
        r, w, lw, k, v, kk, b, g = _rwkv_prep(pr, shift_rows, vecs, mats, e128, vres, grp)
        if l == 0:
            v_first = v
        y_p, h_p = _rwkv_chunked((r, lw, k, v, kk, b), bp, tp, RWKV_CHUNK, RWKV_PAIRS_PER_STEP)
        seqs_s = [a[np_rows:].reshape(bs, ts, N_PAIRS, LANES) for a in (r, w, k, v, kk, b)]
        y_s, st_s = _rwkv_rec(seqs_s, _pack_state(state_wkv[l]), 4, ts)
        y = jnp.concatenate([y_p, y_s.reshape(ns_rows, c)], axis=0)
        rw = _rwkv_post(y, r, k, v, g, rw_ln_w[l].reshape(1, c), rw_ln_b[l].reshape(1, c),
                        rw_r_k[l].reshape(1, c), e128)

        x = _mix_out(att, rw, w_out[l].astype(BF16), x, mod[l], 2, grp)

        wr_pad = jnp.zeros((d, LANES), F32).at[:, :N_EXPERTS].set(w_router[l])
        br_pad = jnp.full((1, LANES), NEG_BIG, F32).at[0, :N_EXPERTS].set(b_router[l])
        x = _moe_layer(x, norm_ffn[l], mod[l], wr_pad, br_pad, w_mlp1, b_mlp1, w_mlp2, b_mlp2, l, grp)

        k_out.append(qkv[:, c:2 * c])
        v_out.append(qkv[:, 2 * c:3 * c])
        shift_p.append(pr[tp - 1:np_rows:tp, :C_SHIFT])
        shift_s.append(pr[np_rows + ts - 1::ts, :C_SHIFT])
        wkv_p.append(_unpack_hbd(h_p.reshape(bp * N_PAIRS, LANES, LANES), bp))
        wkv_s.append(_unpack_state(st_s))

    y = _final_norm(x, norm_final)
    k_all, v_all = jnp.stack(k_out), jnp.stack(v_out)
    pages = tp // PAGE_SIZE
    return (y[:np_rows].reshape(bp, tp, d), y[np_rows:].reshape(bs, ts, d),
            k_all[:, :np_rows].reshape(depth, bp, pages, PAGE_SIZE, N_HEADS, HEAD_DIM),
            v_all[:, :np_rows].reshape(depth, bp, pages, PAGE_SIZE, N_HEADS, HEAD_DIM),
            jnp.stack(shift_p), jnp.stack(wkv_p),
            k_all[:, np_rows:].reshape(depth, bs, ts, N_HEADS, HEAD_DIM),
            v_all[:, np_rows:].reshape(depth, bs, ts, N_HEADS, HEAD_DIM),
            jnp.stack(shift_s), jnp.stack(wkv_s))
```

```python
import functools

import jax
import jax.numpy as jnp
from jax import lax
from jax.experimental import pallas as pl
from jax.experimental.pallas import tpu as pltpu

F32, BF16, I32 = jnp.float32, jnp.bfloat16, jnp.int32

LANES = 128
SUBLANES = 8
VMEM_LIMIT = 50 * 1024 * 1024

D_MODEL = 2048
HEAD_DIM = 64
N_HEADS = 16
C_HEADS = N_HEADS * HEAD_DIM
N_PAIRS = C_HEADS // LANES
MOBA_BLOCK = 256
MOBA_TOPK = 3
PAGE_SIZE = 128
D_DECAY_LORA, D_AAA_LORA, D_GATE_LORA, D_MV_LORA = 64, 64, 160, 32
C_SHIFT = 3 * C_HEADS + D_DECAY_LORA + D_AAA_LORA + D_GATE_LORA
C_SHIFT_PAD = 3584
C_LORA_PAD = C_SHIFT_PAD - 3 * C_HEADS
LN_X_EPS = 64e-5
RMS_EPS = 1e-5
N_EXPERTS = 32
TOP_K = 4
D_EXPERT = D_MODEL
SWIGLU_LIMIT = 7.0
SWIGLU_ALPHA = 1.702
NEG_BIG = -3.0e38

ROW_TILE = 256
SEQ_TABLE_ROWS = 128
EXPERT_TILE = 640
EXPERT_FBLOCK = 256
EXPERT_SUBTILES = 5
RWKV_CHUNK = 64
RWKV_PAIRS_PER_STEP = 8
SAMPLE_PAGES_PER_STEP = 8
PROMPT_PAIRS_PER_STEP = 4


def _cparams(n_axes):
    return pltpu.CompilerParams(dimension_semantics=("arbitrary",) * n_axes,
                                vmem_limit_bytes=VMEM_LIMIT)


def _dot(a, b):
    return jnp.dot(a, b, preferred_element_type=F32)


def _dot_nt(a, b):
    return lax.dot_general(a, b, (((1,), (1,)), ((), ())), preferred_element_type=F32)


def _split2(x):
    hi = x.astype(BF16)
    lo = (x - hi.astype(F32)).astype(BF16)
    return hi, lo


def _split3(x):
    hi = x.astype(BF16)
    r = x - hi.astype(F32)
    mid = r.astype(BF16)
    lo = (r - mid.astype(F32)).astype(BF16)
    return hi, mid, lo


def _dot_ref(a, b, nt=False):
    d = _dot_nt if nt else _dot
    return d(a.astype(BF16), b.astype(BF16))


def _dot_exact_lhs(a_bf16, b):
    bh, bm, bl = _split3(b)
    return _dot(a_bf16, bh) + (_dot(a_bf16, bm) + _dot(a_bf16, bl))


def _seq_onehot(row0, rows, grp):
    n_prompt_rows, tp_shift, ts_shift, n_prompt = grp
    r = row0 + lax.broadcasted_iota(I32, (rows, SEQ_TABLE_ROWS), 0)
    lane = lax.broadcasted_iota(I32, (rows, SEQ_TABLE_ROWS), 1)
    sid = jnp.where(r < n_prompt_rows, r >> tp_shift, n_prompt + ((r - n_prompt_rows) >> ts_shift))
    return jnp.where(lane == sid, 1.0, 0.0).astype(BF16)


def _head_segsum(x, e_ref):
    e = e_ref[...]
    outs = []
    for c in range(x.shape[1] // LANES):
        hi, lo = _split2(x[:, c * LANES:(c + 1) * LANES])
        outs.append(_dot(hi, e) + _dot(lo, e))
    return jnp.concatenate(outs, axis=1)


def _sigmoid(x):
    return 1.0 / (1.0 + jnp.exp(-x))


def _ada_kernel(c_ref, w_ref, b_ref, o_ref):
    o_ref[0] = _dot_ref(c_ref[...], w_ref[0]) + b_ref[0]


def _ada(c_pad, w_ada, b_ada):
    depth, d, n = w_ada.shape
    tn = 512
    return pl.pallas_call(
        _ada_kernel,
        out_shape=jax.ShapeDtypeStruct((depth, SEQ_TABLE_ROWS, n), F32),
        grid=(depth, n // tn),
        in_specs=[pl.BlockSpec((SEQ_TABLE_ROWS, d), lambda l, j: (0, 0)),
                  pl.BlockSpec((1, d, tn), lambda l, j: (l, 0, j)),
                  pl.BlockSpec((1, 1, tn), lambda l, j: (l, 0, j))],
        out_specs=pl.BlockSpec((1, SEQ_TABLE_ROWS, tn), lambda l, j: (l, 0, j)),
        compiler_params=_cparams(2), name="ada_mod",
    )(c_pad, w_ada, b_ada.reshape(depth, 1, n))


def _rms_modulated(x_ref, g_ref, sc_ref, sh_ref, grp):
    rows = x_ref.shape[0]
    x = x_ref[...]
    y = x * lax.rsqrt(jnp.mean(x * x, axis=-1, keepdims=True) + RMS_EPS) * g_ref[...]
    oh = _seq_onehot(pl.program_id(0) * rows, rows, grp)
    sc = _dot_exact_lhs(oh, sc_ref[...])
    sh = _dot_exact_lhs(oh, sh_ref[...])
    return y * (1.0 + sc) + sh


def _rms_mod_kernel(x_ref, g_ref, sc_ref, sh_ref, o_ref, *, grp):
    o_ref[...] = _rms_modulated(x_ref, g_ref, sc_ref, sh_ref, grp).astype(o_ref.dtype)


def _mod_spec(k):
    return pl.BlockSpec((SEQ_TABLE_ROWS, D_MODEL), lambda i, k=k: (0, k))


def _rms_mod(x, g, mod, k_sc, k_sh, grp):
    t = x.shape[0]
    return pl.pallas_call(
        functools.partial(_rms_mod_kernel, grp=grp),
        out_shape=jax.ShapeDtypeStruct((t, D_MODEL), BF16),
        grid=(t // ROW_TILE,),
        in_specs=[pl.BlockSpec((ROW_TILE, D_MODEL), lambda i: (i, 0)),
                  pl.BlockSpec((1, D_MODEL), lambda i: (0, 0)),
                  _mod_spec(k_sc), _mod_spec(k_sh)],
        out_specs=pl.BlockSpec((ROW_TILE, D_MODEL), lambda i: (i, 0)),
        compiler_params=_cparams(1), name="rms_mod",
    )(x, g.reshape(1, D_MODEL), mod, mod)


def _rms_router_kernel(x_ref, g_ref, sc_ref, sh_ref, wr_ref, br_ref, h_ref, route_ref, cnt_ref,
                       carry, *, grp):
    i = pl.program_id(0)
    rows = x_ref.shape[0]
    h = _rms_modulated(x_ref, g_ref, sc_ref, sh_ref, grp)
    h_ref[...] = h
    logits = _dot_ref(h, wr_ref[...]) + br_ref[...]
    lane = lax.broadcasted_iota(I32, (rows, LANES), 1)
    vals, idxs = [], []
    multi = jnp.zeros((rows, LANES), F32)
    for _ in range(TOP_K):
        m = jnp.max(logits, axis=1, keepdims=True)
        sel = jnp.min(jnp.where(logits == m, lane, LANES), axis=1, keepdims=True)
        hit = lane == sel
        vals.append(m)
        idxs.append(sel)
        multi = jnp.where(hit, 1.0, multi)
        logits = jnp.where(hit, NEG_BIG, logits)
    es = [jnp.exp(v - vals[0]) for v in vals]
    denom = es[0] + es[1] + es[2] + es[3]

    @pl.when(i == 0)
    def _():
        carry[...] = jnp.zeros_like(carry)

    r_i = lax.broadcasted_iota(I32, (rows, rows), 0)
    c_i = lax.broadcasted_iota(I32, (rows, rows), 1)
    tri = jnp.where(c_i < r_i, 1.0, 0.0).astype(BF16)
    cum = _dot(tri, multi.astype(BF16)) + carry[0:1, :]
    route = jnp.zeros((rows, LANES), F32)
    for j in range(TOP_K):
        pos = jnp.sum(jnp.where(lane == idxs[j], cum, 0.0), axis=1, keepdims=True)
        route = jnp.where(lane == j, idxs[j].astype(F32), route)
        route = jnp.where(lane == TOP_K + j, es[j] / denom, route)
        route = jnp.where(lane == 2 * TOP_K + j, pos, route)
    route_ref[...] = route
    new_cnt = carry[0:1, :] + jnp.sum(multi, axis=0, keepdims=True)
    carry[...] = jnp.broadcast_to(new_cnt, carry.shape)
    cnt_ref[...] = jnp.broadcast_to(new_cnt, cnt_ref.shape)


def _rms_router(x, g, mod, k_sc, k_sh, wr_pad, br_pad, grp):
    t = x.shape[0]
    return pl.pallas_call(
        functools.partial(_rms_router_kernel, grp=grp),
        out_shape=(jax.ShapeDtypeStruct((t, D_MODEL), F32),
                   jax.ShapeDtypeStruct((t, LANES), F32),
                   jax.ShapeDtypeStruct((SUBLANES, LANES), F32)),
        grid=(t // ROW_TILE,),
        in_specs=[pl.BlockSpec((ROW_TILE, D_MODEL), lambda i: (i, 0)),
                  pl.BlockSpec((1, D_MODEL), lambda i: (0, 0)),
                  _mod_spec(k_sc), _mod_spec(k_sh),
                  pl.BlockSpec((D_MODEL, LANES), lambda i: (0, 0)),
                  pl.BlockSpec((1, LANES), lambda i: (0, 0))],
        out_specs=(pl.BlockSpec((ROW_TILE, D_MODEL), lambda i: (i, 0)),
                   pl.BlockSpec((ROW_TILE, LANES), lambda i: (i, 0)),
                   pl.BlockSpec((SUBLANES, LANES), lambda i: (0, 0))),
        scratch_shapes=[pltpu.VMEM((SUBLANES, LANES), F32)],
        compiler_params=_cparams(1), name="rms_router",
    )(x, g.reshape(1, D_MODEL), mod, mod, wr_pad, br_pad)


def _final_norm_kernel(x_ref, g_ref, o_ref):
    x = x_ref[...]
    o_ref[...] = x * lax.rsqrt(jnp.mean(x * x, axis=-1, keepdims=True) + RMS_EPS) * g_ref[...]


def _final_norm(x, g):
    t = x.shape[0]
    return pl.pallas_call(
        _final_norm_kernel,
        out_shape=jax.ShapeDtypeStruct((t, D_MODEL), F32),
        grid=(t // ROW_TILE,),
        in_specs=[pl.BlockSpec((ROW_TILE, D_MODEL), lambda i: (i, 0)),
                  pl.BlockSpec((1, D_MODEL), lambda i: (0, 0))],
        out_specs=pl.BlockSpec((ROW_TILE, D_MODEL), lambda i: (i, 0)),
        compiler_params=_cparams(1), name="final_norm",
    )(x, g.reshape(1, D_MODEL))


def _mm_kernel(a_ref, w_ref, o_ref):
    o_ref[...] = _dot(a_ref[...], w_ref[...])


def _matmul(a, w, tm, tn):
    m, k = a.shape
    n = w.shape[1]
    return pl.pallas_call(
        _mm_kernel,
        out_shape=jax.ShapeDtypeStruct((m, n), F32),
        grid=(n // tn, m // tm),
        in_specs=[pl.BlockSpec((tm, k), lambda j, i: (i, 0)),
                  pl.BlockSpec((k, tn), lambda j, i: (0, j))],
        out_specs=pl.BlockSpec((tm, tn), lambda j, i: (i, j)),
        compiler_params=_cparams(2), name="proj_matmul",
    )(a, w)


def _mix_out_kernel(att_ref, rw_ref, wa_ref, wb_ref, x_ref, gt_ref, o_ref, *, grp):
    rows = x_ref.shape[0]
    mix = _dot(att_ref[...], wa_ref[...]) + _dot(rw_ref[...], wb_ref[...])
    oh = _seq_onehot(pl.program_id(1) * rows, rows, grp)
    o_ref[...] = x_ref[...] + _dot_exact_lhs(oh, gt_ref[...]) * mix


def _mix_out(att, rw, w_out, x, mod, k_gt, grp):
    t = x.shape[0]
    tm, tn = ROW_TILE, 1024
    nb = D_MODEL // tn
    return pl.pallas_call(
        functools.partial(_mix_out_kernel, grp=grp),
        out_shape=jax.ShapeDtypeStruct((t, D_MODEL), F32),
        grid=(nb, t // tm),
        in_specs=[pl.BlockSpec((tm, C_HEADS), lambda j, i: (i, 0)),
                  pl.BlockSpec((tm, C_HEADS), lambda j, i: (i, 0)),
                  pl.BlockSpec((C_HEADS, tn), lambda j, i: (0, j)),
                  pl.BlockSpec((C_HEADS, tn), lambda j, i: (1, j)),
                  pl.BlockSpec((tm, tn), lambda j, i: (i, j)),
                  pl.BlockSpec((SEQ_TABLE_ROWS, tn), lambda j, i, k=k_gt, nb=nb: (0, k * nb + j))],
        out_specs=pl.BlockSpec((tm, tn), lambda j, i: (i, j)),
        compiler_params=_cparams(2), name="mix_out",
    )(att, rw, w_out, w_out, x, mod)


def _pattn_kernel(q_ref, k_ref, v_ref, o_ref, kmean):
    blk = q_ref.shape[0]
    n_blk = k_ref.shape[0] // blk
    n_pair = q_ref.shape[1] // LANES
    n_pad = kmean.shape[1]
    qi = pl.program_id(2)
    lane = lax.broadcasted_iota(I32, (blk, LANES), 1)
    low = lane < HEAD_DIM
    heads = [(pr, e) for pr in range(n_pair) for e in range(2)]

    @pl.when(qi == 0)
    def _():
        for pr in range(n_pair):
            rows = [jnp.sum(k_ref[n * blk:(n + 1) * blk, pr * LANES:(pr + 1) * LANES], axis=0, keepdims=True)
                    * (1.0 / blk) for n in range(n_blk)]
            if n_pad > n_blk:
                rows.append(jnp.zeros((n_pad - n_blk, LANES), F32))
            kmean[pr] = jnp.concatenate(rows, axis=0)

    blk_i = lax.broadcasted_iota(I32, (n_pad, blk), 0)
    past = jnp.where(blk_i < qi, 1.0, 0.0)
    qh, selm = [], []
    for pr, e in heads:
        q = q_ref[:, pr * LANES:(pr + 1) * LANES] * (HEAD_DIM ** -0.5)
        qm = jnp.where(low, q, 0.0) if e == 0 else jnp.where(low, 0.0, q)
        qh.append(qm.astype(BF16))
        s = _dot_nt(kmean[pr].astype(BF16), qh[-1])
        cnt = jnp.zeros((n_pad, blk), F32)
        for m in range(n_blk):
            sm = s[m:m + 1, :]
            beats = (sm > s) | ((sm == s) & (m < blk_i))
            cnt = cnt + jnp.where(beats, 1.0, 0.0) * jnp.where(m < qi, 1.0, 0.0)
        sel_t = jnp.where(cnt < MOBA_TOPK, past, 0.0)
        sel_t = jnp.concatenate([sel_t, jnp.zeros((LANES - n_pad, blk), F32)], axis=0)
        selm.append(jnp.transpose(sel_t))
    row_i = lax.broadcasted_iota(I32, (blk, blk), 0)
    col_i = lax.broadcasted_iota(I32, (blk, blk), 1)
    causal = jnp.where(col_i <= row_i, 1.0, 0.0)

    def body(n, carry):
        off = pl.multiple_of(n * blk, blk)
        own = jnp.where(n == qi, 1.0, 0.0)
        out = []
        for h, (pr, e) in enumerate(heads):
            kb = k_ref[pl.ds(off, blk), pr * LANES:(pr + 1) * LANES].astype(BF16)
            vb = v_ref[pl.ds(off, blk), pr * LANES:(pr + 1) * LANES].astype(BF16)
            m_old, l_old, acc = carry[3 * h:3 * h + 3]
            logits = _dot_nt(qh[h], kb)
            flag = jnp.sum(jnp.where(lane == n, selm[h], 0.0), axis=1, keepdims=True)
            allowed = (own * causal + (1.0 - own) * flag) > 0.5
            masked = jnp.where(allowed, logits, NEG_BIG)
            m_new = jnp.maximum(m_old, jnp.max(masked, axis=1, keepdims=True))
            alpha = jnp.exp(m_old - m_new)
            p = jnp.where(allowed, jnp.exp(masked - m_new), 0.0)
            l_new = l_old * alpha + jnp.sum(p, axis=1, keepdims=True)
            acc = acc * alpha + _dot(p.astype(BF16), vb)
            out += [m_new, l_new, acc]
        return tuple(out)

    init = (jnp.full((blk, 1), NEG_BIG, F32), jnp.zeros((blk, 1), F32),
            jnp.zeros((blk, LANES), F32)) * len(heads)
    res = lax.fori_loop(0, qi + 1, body, init)
    for pr in range(n_pair):
        a, b = 6 * pr, 6 * pr + 3
        o_ref[:, pr * LANES:(pr + 1) * LANES] = jnp.where(
            low, res[a + 2] / res[a + 1], res[b + 2] / res[b + 1]).astype(o_ref.dtype)


def _prompt_attention(qkv, n_seq, seq_len):
    blk = MOBA_BLOCK
    nq = seq_len // blk
    npair = PROMPT_PAIRS_PER_STEP
    w = npair * LANES
    ng = N_PAIRS // npair
    n_pad = -(-nq // SUBLANES) * SUBLANES
    assert n_pad <= LANES
    return pl.pallas_call(
        _pattn_kernel,
        out_shape=jax.ShapeDtypeStruct((n_seq * seq_len, C_HEADS), BF16),
        grid=(n_seq, ng, nq),
        in_specs=[pl.BlockSpec((blk, w), lambda b, p, i: (b * nq + i, p)),
                  pl.BlockSpec((seq_len, w), lambda b, p, i: (b, ng + p)),
                  pl.BlockSpec((seq_len, w), lambda b, p, i: (b, 2 * ng + p))],
        out_specs=pl.BlockSpec((blk, w), lambda b, p, i: (b * nq + i, p)),
        scratch_shapes=[pltpu.VMEM((npair, n_pad, LANES), F32)],
        compiler_params=_cparams(3), name="moba_prompt",
    )(qkv, qkv, qkv)


def _sattn_k_kernel(pt_ref, qr_ref, *refs, n_grp):
    kp_refs, (lg_ref, sc_ref) = refs[:n_grp], refs[n_grp:]
    j = pl.program_id(1)
    qr = qr_ref[0]
    qb = qr.astype(BF16)
    pages_per_block = MOBA_BLOCK // PAGE_SIZE
    blocks_per_step = n_grp // pages_per_block
    lane = lax.broadcasted_iota(I32, sc_ref.shape[1:], 1)

    @pl.when(j == 0)
    def _():
        sc_ref[0] = jnp.zeros(sc_ref.shape[1:], F32)

    kts = [ref[0] for ref in kp_refs]
    for u in range(n_grp):
        lg_ref[0, u] = _dot(qb, kts[u].astype(BF16))
    sc = sc_ref[0]
    for n in range(blocks_per_step):
        ksum = kts[n * pages_per_block]
        for u in range(1, pages_per_block):
            ksum = ksum + kts[n * pages_per_block + u]
        kmean = jnp.sum(ksum, axis=1, keepdims=True) * (1.0 / MOBA_BLOCK)
        s = _dot(qb, jnp.broadcast_to(kmean, (C_HEADS, LANES)).astype(BF16))
        sc = jnp.where(lane == j * blocks_per_step + n, s, sc)
    sc_ref[0] = sc


def _sattn_v_kernel(pt_ref, lg_ref, sc_ref, *refs, n_new, n_grp, n_blocks):
    vp_refs = refs[:n_grp]
    qr_ref, kn_ref, vn_ref, o_ref, selm, m_s, l_s, acc = refs[n_grp:]
    j = pl.program_id(1)
    nr = lg_ref.shape[2]
    lane = lax.broadcasted_iota(I32, (nr, LANES), 1)
    pages_per_block = MOBA_BLOCK // PAGE_SIZE

    @pl.when(j == 0)
    def _():
        s = sc_ref[0]
        cnt = jnp.zeros((nr, LANES), F32)
        for m in range(n_blocks):
            sm = s[:, m:m + 1]
            cnt = cnt + jnp.where((sm > s) | ((sm == s) & (m < lane)), 1.0, 0.0)
        selm[...] = jnp.where((cnt < MOBA_TOPK) & (lane < n_blocks), 1.0, 0.0)
        m_s[...] = jnp.full(m_s.shape, NEG_BIG, F32)
        l_s[...] = jnp.zeros(l_s.shape, F32)
        acc[...] = jnp.zeros(acc.shape, F32)

    def accumulate(logits, allowed, pv_fns):
        masked = [jnp.where(a, x, NEG_BIG) for x, a in zip(logits, allowed)]
        m_old = m_s[...]
        m_new = m_old
        for x in masked:
            m_new = jnp.maximum(m_new, jnp.max(x, axis=1, keepdims=True))
        alpha = jnp.exp(m_old - m_new)
        l_new = l_s[...] * alpha
        pv = acc[...] * alpha
        for x, a, fn in zip(masked, allowed, pv_fns):
            p = jnp.where(a, jnp.exp(x - m_new), 0.0)
            l_new = l_new + jnp.sum(p, axis=1, keepdims=True)
            pv = pv + fn(p.astype(BF16))
        l_s[...] = l_new
        acc[...] = pv
        m_s[...] = m_new

    sel = selm[...]
    flags = [jnp.sum(jnp.where(lane == (j * n_grp + u) // pages_per_block, sel, 0.0), axis=1, keepdims=True)
             for u in range(n_grp)]
    accumulate([lg_ref[0, u] for u in range(n_grp)],
               [jnp.broadcast_to(f, (nr, LANES)) > 0.0 for f in flags],
               [functools.partial(lambda p, ref: _dot_nt(p, ref[0].astype(BF16)), ref=ref) for ref in vp_refs])

    @pl.when(j == pl.num_programs(1) - 1)
    def _():
        pad = jnp.zeros((LANES - n_new, C_HEADS), F32)
        kn = jnp.concatenate([kn_ref[...], pad], axis=0).astype(BF16)
        vn = jnp.concatenate([vn_ref[...], pad], axis=0).astype(BF16)
        logits = _dot_nt(qr_ref[0].astype(BF16), kn)
        row = lax.broadcasted_iota(I32, (nr, LANES), 0)
        accumulate([logits], [(lane <= (row & (n_new - 1))) & (lane < n_new)], [lambda p: _dot(p, vn)])
        out = acc[...] / l_s[...]
        r2 = lax.broadcasted_iota(I32, (nr, C_HEADS), 0)
        c2 = lax.broadcasted_iota(I32, (nr, C_HEADS), 1)
        own = jnp.where((r2 >> _log2(n_new)) == (c2 >> _log2(HEAD_DIM)), out, 0.0)
        o_ref[...] = jnp.sum(own.reshape(N_HEADS, n_new, C_HEADS), axis=0)


def _sample_attention(qkv, row0, n_seq, n_new, cache_kt, cache_vt, page_table, page_base):
    n_pages = page_table.shape[1]
    n_grp = SAMPLE_PAGES_PER_STEP
    n_blocks = (n_pages * PAGE_SIZE) // MOBA_BLOCK
    assert n_pages % n_grp == 0 and n_grp % (MOBA_BLOCK // PAGE_SIZE) == 0 and n_blocks <= LANES
    n_steps = n_pages // n_grp
    nr = N_HEADS * n_new
    q = qkv[row0:row0 + n_seq * n_new, :C_HEADS].reshape(n_seq, 1, n_new, C_HEADS) * (HEAD_DIM ** -0.5)
    head_of_col = (jnp.arange(C_HEADS, dtype=I32) // HEAD_DIM)[None, None, None, :]
    head_of_row = jnp.arange(N_HEADS, dtype=I32)[None, :, None, None]
    qrows = jnp.where(head_of_row == head_of_col, q, 0.0).reshape(n_seq, nr, C_HEADS)
    pt = (page_table + page_base).astype(I32)
    page_specs = [pl.BlockSpec((1, C_HEADS, PAGE_SIZE), lambda b, j, pt, u=u: (pt[b, j * n_grp + u], 0, 0))
                  for u in range(n_grp)]
    lg_spec = pl.BlockSpec((1, n_grp, nr, PAGE_SIZE), lambda b, j, pt: (b, j, 0, 0))
    sc_spec = pl.BlockSpec((1, nr, LANES), lambda b, j, pt: (b, 0, 0))
    qr_spec = pl.BlockSpec((1, nr, C_HEADS), lambda b, j, pt: (b, 0, 0))

    logits, scores = pl.pallas_call(
        functools.partial(_sattn_k_kernel, n_grp=n_grp),
        out_shape=(jax.ShapeDtypeStruct((n_seq, n_pages, nr, PAGE_SIZE), F32),
                   jax.ShapeDtypeStruct((n_seq, nr, LANES), F32)),
        grid_spec=pltpu.PrefetchScalarGridSpec(
            num_scalar_prefetch=1, grid=(n_seq, n_steps),
            in_specs=[qr_spec] + page_specs,
            out_specs=(lg_spec, sc_spec)),
        compiler_params=_cparams(2), name="moba_sample_k",
    )(pt, qrows, *([cache_kt] * n_grp))

    rb = row0 // n_new
    return pl.pallas_call(
        functools.partial(_sattn_v_kernel, n_new=n_new, n_grp=n_grp, n_blocks=n_blocks),
        out_shape=jax.ShapeDtypeStruct((n_seq * n_new, C_HEADS), F32),
        grid_spec=pltpu.PrefetchScalarGridSpec(
            num_scalar_prefetch=1, grid=(n_seq, n_steps),
            in_specs=[lg_spec, sc_spec] + page_specs + [
                qr_spec,
                pl.BlockSpec((n_new, C_HEADS), lambda b, j, pt: (rb + b, 1)),
                pl.BlockSpec((n_new, C_HEADS), lambda b, j, pt: (rb + b, 2))],
            out_specs=pl.BlockSpec((n_new, C_HEADS), lambda b, j, pt: (b, 0)),
            scratch_shapes=[pltpu.VMEM((nr, LANES), F32), pltpu.VMEM((nr, 1), F32),
                            pltpu.VMEM((nr, 1), F32), pltpu.VMEM((nr, C_HEADS), F32)]),
        compiler_params=_cparams(2), name="moba_sample_v",
    )(pt, logits, scores, *([cache_vt] * n_grp), qrows, qkv, qkv)


def _rwkv_prep_kernel(*refs, has_vres, grp):
    if has_vres:
        (pr_ref, edge_ref, s0_ref, mu_ref, w0_ref, a0_ref, wup_ref, aup_ref, gup_ref, kk_ref_, ka_ref,
         e_ref, vf_ref, v0_ref, vdn_ref, vup_ref,
         r_o, w_o, lw_o, k_o, v_o, kk_o, b_o, g_o) = refs
    else:
        (pr_ref, edge_ref, s0_ref, mu_ref, w0_ref, a0_ref, wup_ref, aup_ref, gup_ref, kk_ref_, ka_ref,
         e_ref, r_o, w_o, lw_o, k_o, v_o, kk_o, b_o, g_o) = refs
    n_prompt_rows, tp_shift, ts_shift, _ = grp
    pr = pr_ref[...]
    tm = pr.shape[0]
    loc = lax.broadcasted_iota(I32, (tm, 1), 0)
    glob = pl.program_id(0) * tm + loc
    prev = jnp.where(loc == 0, edge_ref[SUBLANES - 1:SUBLANES, :], pltpu.roll(pr, 1, 0))
    first_p = (glob < n_prompt_rows) & ((glob & ((1 << tp_shift) - 1)) == 0)
    first_s = (glob >= n_prompt_rows) & (((glob - n_prompt_rows) & ((1 << ts_shift) - 1)) == 0)
    prev = jnp.where(first_p, 0.0, jnp.where(first_s, s0_ref[...], prev))
    xm = pr + (prev - pr) * mu_ref[...]
    c = C_HEADS
    r, k, v, lora = xm[:, :c], xm[:, c:2 * c], xm[:, 2 * c:3 * c], xm[:, 3 * c:]
    z = w0_ref[...] + _dot_ref(jnp.tanh(lora), wup_ref[...])
    w_log = -(jnp.maximum(-z, 0.0) + jnp.log(1.0 + jnp.exp(-jnp.abs(z)))) - 0.5
    log_decay = -jnp.exp(w_log)
    decay = jnp.exp(log_decay)
    a = _sigmoid(a0_ref[...] + _dot_ref(lora, aup_ref[...]))
    g = _dot_ref(_sigmoid(lora), gup_ref[...])
    if has_vres:
        gate = _sigmoid(v0_ref[...] + _dot_ref(_dot_ref(v, vdn_ref[...]), vup_ref[...]))
        v = v + (vf_ref[...] - v) * gate
    kk = k * kk_ref_[...]
    ss = _head_segsum(kk * kk, e_ref)
    kk = kk * lax.rsqrt(jnp.maximum(ss, 1e-24))
    r_o[...] = r
    w_o[...] = decay
    lw_o[...] = log_decay
    k_o[...] = k * (1.0 + (a - 1.0) * ka_ref[...])
    v_o[...] = v
    kk_o[...] = kk
    b_o[...] = kk * a
    g_o[...] = g


def _rwkv_prep(pr, shift_rows, vecs, mats, e128, vres, grp):
    t = pr.shape[0]
    tm = 128
    c = C_HEADS
    assert grp[0] % tm == 0 and (t - grp[0]) % tm == 0
    n_prompt_tiles = grp[0] // tm
    row = lambda w: pl.BlockSpec((tm, w), lambda i: (i, 0))
    vec = lambda w: pl.BlockSpec((1, w), lambda i: (0, 0))
    full = lambda a: pl.BlockSpec(a.shape, lambda i: (0, 0))
    edge = pl.BlockSpec((SUBLANES, C_SHIFT_PAD), lambda i: (jnp.maximum(i * (tm // SUBLANES) - 1, 0), 0))
    s0 = pl.BlockSpec((tm, C_SHIFT_PAD), lambda i: (jnp.maximum(i - n_prompt_tiles, 0), 0))
    mu, w0, a0, k_k, k_a = vecs
    wup, aup, gup = mats
    args = [pr, pr, shift_rows, mu, w0, a0, wup, aup, gup, k_k, k_a, e128]
    specs = [row(C_SHIFT_PAD), edge, s0, vec(C_SHIFT_PAD), vec(c), vec(c),
             full(wup), full(aup), full(gup), vec(c), vec(c), full(e128)]
    if vres is not None:
        v_first, v0, vdn, vup = vres
        args += [v_first, v0, vdn, vup]
        specs += [row(c), vec(c), full(vdn), full(vup)]
    return pl.pallas_call(
        functools.partial(_rwkv_prep_kernel, has_vres=vres is not None, grp=grp),
        out_shape=tuple(jax.ShapeDtypeStruct((t, c), F32) for _ in range(8)),
        grid=(t // tm,),
        in_specs=specs,
        out_specs=tuple(row(c) for _ in range(8)),
        compiler_params=_cparams(1), name="rwkv_prep",
    )(*args)


def _rwkv_rec_kernel(r_ref, w_ref, k_ref, v_ref, kk_ref, b_ref, s0_ref, y_ref, st_ref, state):
    nb, tc = r_ref.shape[:2]
    c = pl.program_id(1)

    @pl.when(c == 0)
    def _():
        state[...] = s0_ref[...]

    lane = lax.broadcasted_iota(I32, (HEAD_DIM, LANES), 1)
    row = lax.broadcasted_iota(I32, (HEAD_DIM, LANES), 0)
    low = lane < HEAD_DIM
    eye = jnp.where(low, lane, lane - HEAD_DIM) == row

    def seg(x):
        e = jnp.sum(jnp.where(low, x, 0.0), axis=1, keepdims=True)
        o = jnp.sum(jnp.where(low, 0.0, x), axis=1, keepdims=True)
        return jnp.where(low, e, o)

    def step(t, carry):
        for bi in range(nb):
            r_t, w_t, k_t, v_t, kk_t, b_t = (ref[bi, t] for ref in
                                             (r_ref, w_ref, k_ref, v_ref, kk_ref, b_ref))
            y_rows = []
            for p in range(N_PAIRS):
                s_old = state[bi, p]
                sa = seg(s_old * (-kk_t[p:p + 1, :]))
                v_col = seg(jnp.where(eye, v_t[p:p + 1, :], 0.0))
                s_new = s_old * w_t[p:p + 1, :] + sa * b_t[p:p + 1, :] + v_col * k_t[p:p + 1, :]
                state[bi, p] = s_new
                y_b = seg(s_new * r_t[p:p + 1, :])
                y_rows.append(jnp.sum(jnp.where(eye, y_b, 0.0), axis=0, keepdims=True))
            y_ref[bi, t] = jnp.concatenate(y_rows, axis=0)
        return carry

    lax.fori_loop(0, tc, step, 0)

    @pl.when(c == pl.num_programs(1) - 1)
    def _():
        st_ref[...] = state[...]


def _rwkv_rec(seqs, s0, nb, tc):
    b, t = seqs[0].shape[:2]
    blk = pl.BlockSpec((nb, tc, N_PAIRS, LANES), lambda g, i: (g, i, 0, 0))
    st = pl.BlockSpec((nb, N_PAIRS, HEAD_DIM, LANES), lambda g, i: (g, 0, 0, 0))
    return pl.pallas_call(
        _rwkv_rec_kernel,
        out_shape=(jax.ShapeDtypeStruct((b, t, N_PAIRS, LANES), F32),
                   jax.ShapeDtypeStruct((b, N_PAIRS, HEAD_DIM, LANES), F32)),
        grid=(b // nb, t // tc),
        in_specs=[blk] * 6 + [st],
        out_specs=(blk, st),
        scratch_shapes=[pltpu.VMEM((nb, N_PAIRS, HEAD_DIM, LANES), F32)],
        compiler_params=_cparams(2), name="rwkv_rec",
    )(*seqs, s0)


def _sp(x):
    return _split2(x)


def _mm3(a, b, mode="nn"):
    if mode == "nt":
        d = _dot_nt
    elif mode == "tn":
        d = lambda x, y: lax.dot_general(x, y, (((0,), (0,)), ((), ())), preferred_element_type=F32)
    else:
        d = _dot
    return d(a[0], b[0]) + (d(a[0], b[1]) + d(a[1], b[0]))


def _rwkv_chunk_kernel(r_ref, lw_ref, k_ref, v_ref, kk_ref, b_ref, y_ref, st_ref, hbd, *, pairs):
    c = pl.program_id(2)
    n_c = r_ref.shape[0]
    n2 = 2 * n_c
    log_c = _log2(n_c)

    @pl.when(c == 0)
    def _():
        hbd[...] = jnp.zeros(hbd.shape, F32)

    lane = lax.broadcasted_iota(I32, (n_c, LANES), 1)
    low = lane < HEAD_DIM
    i = lax.broadcasted_iota(I32, (n2, n2), 0)
    j = lax.broadcasted_iota(I32, (n2, n2), 1)
    same = (i >> log_c) == (j >> log_c)
    ti = i & (n_c - 1)
    tj = j & (n_c - 1)
    m_stril = jnp.where(same & (tj < ti), 1.0, 0.0)
    m_tril = jnp.where(same & (tj <= ti), 1.0, 0.0)
    eye = jnp.where(i == j, 1.0, 0.0)
    m_blk = {s: jnp.where((i >> s) == (j >> s), 1.0, 0.0) for s in range(3, log_c + 1)}
    ci = lax.broadcasted_iota(I32, (n_c, n_c), 0)
    cj = lax.broadcasted_iota(I32, (n_c, n_c), 1)
    tri_c = jnp.where(cj <= ci, 1.0, 0.0).astype(BF16)

    def stack(x):
        return jnp.concatenate([jnp.where(low, x, 0.0), jnp.where(low, 0.0, x)], axis=0)

    def each(fn, *cols):
        return [fn(*args) for args in zip(*cols)]

    sls = [slice(q * LANES, (q + 1) * LANES) for q in range(pairs)]
    lw = [lw_ref[:, sl] for sl in sls]
    cum = each(lambda x: _dot_exact_lhs(tri_c, x), lw)
    cum_c = each(lambda x: x[n_c - 1:n_c, :], cum)
    e_neg = each(lambda x: jnp.exp(-x), cum)
    e_tail = each(lambda x, xc: jnp.exp(xc - x), cum, cum_c)
    kk = [kk_ref[:, sl] for sl in sls]
    b = [b_ref[:, sl] for sl in sls]
    k = [k_ref[:, sl] for sl in sls]
    a_s = each(lambda x, cu, l: _sp(stack(-x * jnp.exp(cu - l))), kk, cum, lw)
    b_s = each(lambda x, e: _sp(stack(x * e)), b, e_neg)
    k_s = each(lambda x, e: _sp(stack(x * e)), k, e_neg)
    r_st = [stack(r_ref[:, sl] * jnp.exp(cu)) for sl, cu in zip(sls, cum)]
    r_s = each(_sp, r_st)
    bh_s = each(lambda x, e: _sp(stack(x * e)), b, e_tail)
    kh_s = each(lambda x, e: _sp(stack(x * e)), k, e_tail)
    v_s = [_sp(stack(v_ref[:, sl])) for sl in sls]

    l_ab = each(lambda x, y: m_stril * _mm3(x, y, "nt"), a_s, b_s)
    l_ak = each(lambda x, y: _sp(m_stril * _mm3(x, y, "nt")), a_s, k_s)
    m_rb = each(lambda x, y: _sp(m_tril * _mm3(x, y, "nt")), r_s, b_s)
    m_rk = each(lambda x, y: _sp(m_tril * _mm3(x, y, "nt")), r_s, k_s)

    d1 = each(lambda x: x * m_blk[3], l_ab)
    d1s = each(_sp, d1)
    d2 = each(lambda x: _mm3(x, x), d1s)
    d2s = each(_sp, d2)
    d4 = each(lambda x: _mm3(x, x), d2s)
    i12 = each(lambda x, y: _sp(_mm3(_sp(eye + x), _sp(eye + y))), d1, d2)
    inv = each(lambda x, y: _mm3(x, _sp(eye + y)), i12, d4)
    for s in range(4, log_c + 1):
        l_m = each(lambda x: _sp(x * (m_blk[s] - m_blk[s - 1])), l_ab)
        inv_s = each(_sp, inv)
        t_m = each(lambda x, y: _sp(_mm3(x, y)), inv_s, l_m)
        inv = each(lambda x, y, z: x + _mm3(y, z), inv, t_m, inv_s)
    inv_s = each(_sp, inv)
    w1 = each(lambda x, y: _sp(_mm3(x, y)), inv_s, a_s)
    lv = each(lambda x, y: _sp(_mm3(x, y)), l_ak, v_s)
    u0 = each(lambda x, y: _sp(_mm3(x, y)), inv_s, lv)
    y1 = each(lambda x, y, z: x + _mm3(y, z), r_st, m_rb, w1)
    y0 = each(lambda x, y, z, u: _mm3(x, y) + _mm3(z, u), m_rb, u0, m_rk, v_s)
    g = each(lambda xc, x, y: _sp(eye * jnp.exp(xc) + _mm3(x, y, "tn")), cum_c, bh_s, w1)
    h_add = each(lambda x, y, z, u: _mm3(x, y, "tn") + _mm3(z, u, "tn"), bh_s, u0, kh_s, v_s)
    y1c = each(lambda x: _sp(x[:n_c] + x[n_c:]), y1)
    y0c = each(lambda x: x[:n_c] + x[n_c:], y0)

    h_s = [_sp(hbd[q]) for q in range(pairs)]
    for q in range(pairs):
        y_ref[:, sls[q]] = _mm3(y1c[q], h_s[q]) + y0c[q]
    for q in range(pairs):
        hbd[q] = _mm3(g[q], h_s[q]) + h_add[q]

    @pl.when(c == pl.num_programs(2) - 1)
    def _():
        st_ref[0] = hbd[...]


def _rwkv_chunked(seqs, n_seq, seq_len, chunk, pairs):
    n_chunks = seq_len // chunk
    n_grp = N_PAIRS // pairs
    blk = pl.BlockSpec((chunk, pairs * LANES), lambda b, p, c: (b * n_chunks + c, p))
    return pl.pallas_call(
        functools.partial(_rwkv_chunk_kernel, pairs=pairs),
        out_shape=(jax.ShapeDtypeStruct((n_seq * seq_len, C_HEADS), F32),
                   jax.ShapeDtypeStruct((n_seq * n_grp, pairs, LANES, LANES), F32)),
        grid=(n_seq, n_grp, n_chunks),
        in_specs=[blk] * 6,
        out_specs=(blk, pl.BlockSpec((1, pairs, LANES, LANES), lambda b, p, c: (b * n_grp + p, 0, 0, 0))),
        scratch_shapes=[pltpu.VMEM((pairs, LANES, LANES), F32)],
        compiler_params=_cparams(3), name="rwkv_chunked",
    )(*seqs)


def _unpack_hbd(h, n_seq):
    h = h.reshape(n_seq, N_PAIRS, 2, HEAD_DIM, 2, HEAD_DIM)
    diag = jnp.stack([h[:, :, 0, :, 0, :], h[:, :, 1, :, 1, :]], axis=2)
    return jnp.transpose(diag, (0, 1, 2, 4, 3)).reshape(n_seq, N_HEADS, HEAD_DIM, HEAD_DIM)


def _rwkv_post_kernel(y_ref, r_ref, k_ref, v_ref, g_ref, lnw_ref, lnb_ref, rk_ref, e_ref, o_ref):
    y = y_ref[...]
    inv = 1.0 / HEAD_DIM
    mean = _head_segsum(y, e_ref) * inv
    yc = y - mean
    var = _head_segsum(yc * yc, e_ref) * inv
    yn = yc * lax.rsqrt(var + LN_X_EPS) * lnw_ref[...] + lnb_ref[...]
    bonus = _head_segsum(r_ref[...] * k_ref[...] * rk_ref[...], e_ref) * v_ref[...]
    o_ref[...] = ((yn + bonus) * g_ref[...]).astype(o_ref.dtype)


def _rwkv_post(y, r, k, v, g, ln_w, ln_b, r_k, e128):
    t, c = y.shape
    row = pl.BlockSpec((ROW_TILE, c), lambda i: (i, 0))
    vec = pl.BlockSpec((1, c), lambda i: (0, 0))
    return pl.pallas_call(
        _rwkv_post_kernel,
        out_shape=jax.ShapeDtypeStruct((t, c), BF16),
        grid=(t // ROW_TILE,),
        in_specs=[row] * 5 + [vec] * 3 + [pl.BlockSpec(e128.shape, lambda i: (0, 0))],
        out_specs=row,
        compiler_params=_cparams(1), name="rwkv_post",
    )(y, r, k, v, g, ln_w, ln_b, r_k, e128)


def _pack_state(s):
    b = s.shape[0]
    s = s.reshape(b, N_PAIRS, 2, HEAD_DIM, HEAD_DIM)
    return jnp.transpose(s, (0, 1, 3, 2, 4)).reshape(b, N_PAIRS, HEAD_DIM, LANES)


def _unpack_state(s):
    b = s.shape[0]
    s = s.reshape(b, N_PAIRS, HEAD_DIM, 2, HEAD_DIM)
    return jnp.transpose(s, (0, 1, 3, 2, 4)).reshape(b, N_HEADS, HEAD_DIM, HEAD_DIM)


def _expert_kernel(src_ref, te_ref, nu_ref, h_ref, w1_ref, b1_ref, w2_ref, b2_ref, sel_ref, o_ref,
                   xbuf, xb, sems):
    t = pl.program_id(0)
    j = pl.program_id(1)
    nf = pl.num_programs(1)
    tm = o_ref.shape[0]
    per_step = tm // nf

    def start_row(tile, slot, r, priority):
        pltpu.make_async_copy(h_ref.at[pl.ds(src_ref[tile * tm + r], 1), :],
                              xbuf.at[slot, pl.ds(r, 1), :], sems.at[slot]).start(priority=priority)

    @pl.when(t < nu_ref[0])
    def _():
        @pl.when((t == 0) & (j == 0))
        def _():
            def first(r, carry):
                start_row(0, 0, r, 0)
                return carry
            lax.fori_loop(0, tm, first, 0)

        def wait_tile(tile):
            slot = tile % 2
            pltpu.make_async_copy(h_ref.at[pl.ds(0, tm), :], xbuf.at[slot], sems.at[slot]).wait()

        @pl.when(j == 0)
        def _():
            wait_tile(t)
            xb[...] = xbuf[t % 2].astype(BF16)

        nxt_tile = jnp.minimum(t + 1, nu_ref[0] - 1)
        for r in range(per_step):
            start_row(nxt_tile, (t + 1) % 2, j * per_step + r, r % 2)

        w1b = w1_ref[0].astype(BF16)
        w2b = w2_ref[0].astype(BF16)
        sub = tm // EXPERT_SUBTILES
        rows = [slice(s * sub, (s + 1) * sub) for s in range(EXPERT_SUBTILES)]
        hh = [_dot(xb[rs, :], w1b) + b1_ref[0] for rs in rows]
        width = hh[0].shape[1]
        nxt = [pltpu.roll(h, width - 1, 1) for h in hh]
        glu = [jnp.minimum(h, SWIGLU_LIMIT) for h in hh]
        act = [g * _sigmoid(SWIGLU_ALPHA * g) * (jnp.clip(n, -SWIGLU_LIMIT, SWIGLU_LIMIT) + 1.0)
               for g, n in zip(glu, nxt)]
        act = [_dot(a.astype(BF16), sel_ref[...]).astype(BF16) for a in act]
        part = [_dot(a, w2b) for a in act]

        @pl.when(j == 0)
        def _():
            for rs, p in zip(rows, part):
                o_ref[rs, :] = p + b2_ref[0]

        @pl.when(j > 0)
        def _():
            for rs, p in zip(rows, part):
                o_ref[rs, :] = o_ref[rs, :] + p

        @pl.when((t == nu_ref[0] - 1) & (j == nf - 1))
        def _():
            wait_tile(t + 1)

    @pl.when((t >= nu_ref[0]) & (j == 0))
    def _():
        o_ref[...] = jnp.zeros(o_ref.shape, F32)


def _expert_mlp(h, src, tile_expert, n_used, w1, b1, w2, b2):
    n_rows = src.shape[0]
    tm, bf = EXPERT_TILE, EXPERT_FBLOCK
    n_tiles = n_rows // tm
    nf = D_EXPERT // bf
    assert tm % nf == 0 and tm % (EXPERT_SUBTILES * 16) == 0
    sel = (jnp.arange(2 * bf, dtype=I32)[:, None] == 2 * jnp.arange(bf, dtype=I32)[None, :]).astype(BF16)
    return pl.pallas_call(
        _expert_kernel,
        out_shape=jax.ShapeDtypeStruct((n_rows, D_MODEL), F32),
        grid_spec=pltpu.PrefetchScalarGridSpec(
            num_scalar_prefetch=3, grid=(n_tiles, nf),
            in_specs=[pl.BlockSpec(memory_space=pl.ANY),
                      pl.BlockSpec((1, D_MODEL, 2 * bf), lambda t, j, src, te, nu: (te[t], 0, j)),
                      pl.BlockSpec((1, 1, 2 * bf), lambda t, j, src, te, nu: (te[t], 0, j)),
                      pl.BlockSpec((1, bf, D_MODEL), lambda t, j, src, te, nu: (te[t], j, 0)),
                      pl.BlockSpec((1, 1, D_MODEL), lambda t, j, src, te, nu: (te[t], 0, 0)),
                      pl.BlockSpec((2 * bf, bf), lambda t, j, src, te, nu: (0, 0))],
            out_specs=pl.BlockSpec((tm, D_MODEL), lambda t, j, src, te, nu: (t, 0)),
            scratch_shapes=[pltpu.VMEM((2, tm, D_MODEL), F32), pltpu.VMEM((tm, D_MODEL), BF16),
                            pltpu.SemaphoreType.DMA((2,))]),
        compiler_params=_cparams(2), name="expert_mlp",
    )(src, tile_expert, n_used, h, w1.reshape(-1, D_MODEL, 2 * D_EXPERT), b1.reshape(-1, 1, 2 * D_EXPERT),
      w2.reshape(-1, D_EXPERT, D_MODEL), b2.reshape(-1, 1, D_MODEL), sel)


def _combine_kernel(slot_ref, ys_ref, x_ref, route_ref, gt_ref, o_ref, ybuf, sem, *, grp):
    rows = x_ref.shape[0]
    base = pl.program_id(0) * rows

    def copy(r, j):
        return pltpu.make_async_copy(ys_ref.at[pl.ds(slot_ref[(base + r) * TOP_K + j], 1), :],
                                     ybuf.at[j, pl.ds(r, 1), :], sem)

    def start(r, carry):
        for j in range(TOP_K):
            copy(r, j).start(priority=j % 2)
        return carry

    def wait(r, carry):
        for j in range(TOP_K):
            copy(r, j).wait()
        return carry

    lax.fori_loop(0, rows, start, 0, unroll=4)
    lax.fori_loop(0, rows, wait, 0, unroll=4)
    route = route_ref[...]
    moe = jnp.zeros(x_ref.shape, F32)
    for j in range(TOP_K):
        moe = moe + route[:, TOP_K + j:TOP_K + j + 1] * ybuf[j]
    oh = _seq_onehot(base, rows, grp)
    o_ref[...] = x_ref[...] + _dot_exact_lhs(oh, gt_ref[...]) * moe


def _combine(x, ys, slots, route, mod, k_gt, grp):
    t = x.shape[0]
    tm = 128
    return pl.pallas_call(
        functools.partial(_combine_kernel, grp=grp),
        out_shape=jax.ShapeDtypeStruct((t, D_MODEL), F32),
        grid_spec=pltpu.PrefetchScalarGridSpec(
            num_scalar_prefetch=1, grid=(t // tm,),
            in_specs=[pl.BlockSpec(memory_space=pl.ANY),
                      pl.BlockSpec((tm, D_MODEL), lambda i, s: (i, 0)),
                      pl.BlockSpec((tm, LANES), lambda i, s: (i, 0)),
                      pl.BlockSpec((SEQ_TABLE_ROWS, D_MODEL), lambda i, s, k=k_gt: (0, k))],
            out_specs=pl.BlockSpec((tm, D_MODEL), lambda i, s: (i, 0)),
            scratch_shapes=[pltpu.VMEM((TOP_K, tm, D_MODEL), F32), pltpu.SemaphoreType.DMA]),
        compiler_params=_cparams(1), name="moe_combine",
    )(slots, ys, x, route, mod)


def _moe_layer(x, g, mod, wr_pad, br_pad, w1, b1, w2, b2, layer, grp):
    t = x.shape[0]
    h2, route, counts = _rms_router(x, g, mod, 4, 3, wr_pad, br_pad, grp)
    tm = EXPERT_TILE
    n_tiles = (t * TOP_K) // tm + N_EXPERTS
    cnt = counts[0, :N_EXPERTS].astype(I32)
    tiles_per = (cnt + tm - 1) // tm
    tile_end = jnp.cumsum(tiles_per)
    group_start = (tile_end - tiles_per) * tm
    idx = route[:, :TOP_K].astype(I32)
    pos = route[:, 2 * TOP_K:3 * TOP_K].astype(I32)
    slots = (group_start[idx] + pos).reshape(-1)
    token_of_pair = jnp.repeat(jnp.arange(t, dtype=I32), TOP_K)
    src = jnp.zeros((n_tiles * tm,), I32).at[slots].set(token_of_pair)
    n_used = tile_end[-1:].astype(I32)
    tile_expert = jnp.searchsorted(tile_end, jnp.arange(n_tiles, dtype=I32), side="right").astype(I32)
    tile_expert = jnp.minimum(tile_expert, N_EXPERTS - 1)
    tile_expert = jnp.where(jnp.arange(n_tiles) < n_used[0], tile_expert,
                            tile_expert[jnp.maximum(n_used[0] - 1, 0)])
    ys = _expert_mlp(h2, src, tile_expert + layer * N_EXPERTS, n_used, w1, b1, w2, b2)
    return _combine(x, ys, slots, route, mod, 5, grp)


def _log2(n):
    s = n.bit_length() - 1
    assert (1 << s) == n, "sequence lengths must be powers of two"
    return s


def kernel(x_prompt, x_sample, cache_k, cache_v, page_table, state_shift, state_wkv, c_prompt, c_sample,
           w_ada, b_ada, norm_mix, norm_ffn, w_in, w_out, rw_mu, rw_w0, rw_w_up, rw_a0, rw_a_up,
           rw_g_up, rw_k_k, rw_k_a, rw_r_k, rw_ln_w, rw_ln_b, rw_v0, rw_v_down, rw_v_up,
           w_router, b_router, w_mlp1, b_mlp1, w_mlp2, b_mlp2, norm_final):
    bp, tp, d = x_prompt.shape
    bs, ts, _ = x_sample.shape
    depth = w_ada.shape[0]
    n_pool = cache_k.shape[1]
    np_rows, ns_rows = bp * tp, bs * ts
    t_all = np_rows + ns_rows
    assert d == D_MODEL and t_all % ROW_TILE == 0 and np_rows % ROW_TILE == 0
    assert bp + bs <= SEQ_TABLE_ROWS and tp % MOBA_BLOCK == 0 and ts <= SUBLANES
    grp = (np_rows, _log2(tp), _log2(ts), bp)
    c = C_HEADS

    x = jnp.concatenate([x_prompt.reshape(np_rows, d), x_sample.reshape(ns_rows, d)], axis=0)
    c_pad = jnp.zeros((SEQ_TABLE_ROWS, d), F32).at[:bp + bs].set(jnp.concatenate([c_prompt, c_sample], axis=0))
    mod = _ada(c_pad, w_ada, b_ada)

    cache_kt = jnp.transpose(cache_k, (0, 1, 3, 4, 2)).reshape(depth * n_pool, c, PAGE_SIZE)
    cache_vt = jnp.transpose(cache_v, (0, 1, 3, 4, 2)).reshape(depth * n_pool, c, PAGE_SIZE)
    e128 = (jnp.arange(LANES)[:, None] // HEAD_DIM == jnp.arange(LANES)[None, :] // HEAD_DIM).astype(BF16)

    def lora_pad(w, start):
        return jnp.zeros((C_LORA_PAD, c), F32).at[start:start + w.shape[0]].set(w)

    k_out, v_out, shift_p, shift_s, wkv_p, wkv_s = [], [], [], [], [], []
    v_first = None
    for l in range(depth):
        h = _rms_mod(x, norm_mix[l], mod[l], 1, 0, grp)
        w_qkv = w_in[l][:, :3 * c].astype(BF16)
        w_pr = jnp.zeros((d, C_SHIFT_PAD), BF16).at[:, :C_SHIFT].set(w_in[l][:, 3 * c:].astype(BF16))
        qkv = _matmul(h, w_qkv, 768, 1024)
        pr = _matmul(h, w_pr, 768, 512)

        att_p = _prompt_attention(qkv, bp, tp)
        att_s = _sample_attention(qkv, np_rows, bs, ts, cache_kt, cache_vt, page_table, l * n_pool)
        att = jnp.concatenate([att_p, att_s.astype(BF16)], axis=0)

        shift_rows = jnp.zeros((bs, ts, C_SHIFT_PAD), F32).at[:, 0, :C_SHIFT].set(state_shift[l])
        shift_rows = shift_rows.reshape(ns_rows, C_SHIFT_PAD)
        mu =jnp.zeros((1, C_SHIFT_PAD), F32).at[0, :C_SHIFT].set(rw_mu[l])
        vecs = (mu, rw_w0[l].reshape(1, c), rw_a0[l].reshape(1, c), rw_k_k[l].reshape(1, c),
                rw_k_a[l].reshape(1, c))
        mats = (lora_pad(rw_w_up[l], 0), lora_pad(rw_a_up[l], D_DECAY_LORA),
                lora_pad(rw_g_up[l], D_DECAY_LORA + D_AAA_LORA))
        vres = None
        if l > 0:
            vdn = jnp.zeros((c, LANES), F32).at[:, :D_MV_LORA].set(rw_v_down[l - 1])
            vup = jnp.zeros((LANES, c), F32).at[:D_MV_LORA].set(rw_v_up[l - 1])
            vres = (v_first, rw_v0[l - 1].reshape(1, c), vdn, vup)
        r, w, lw, k, v, kk, b, g = _rwkv_prep(pr, shift_rows, vecs, mats, e128, vres, grp)
        if l == 0:
            v_first = v
        y_p, h_p = _rwkv_chunked((r, lw, k, v, kk, b), bp, tp, RWKV_CHUNK, RWKV_PAIRS_PER_STEP)
        seqs_s = [a[np_rows:].reshape(bs, ts, N_PAIRS, LANES) for a in (r, w, k, v, kk, b)]
        y_s, st_s = _rwkv_rec(seqs_s, _pack_state(state_wkv[l]), 4, ts)
        y = jnp.concatenate([y_p, y_s.reshape(ns_rows, c)], axis=0)
        rw = _rwkv_post(y, r, k, v, g, rw_ln_w[l].reshape(1, c), rw_ln_b[l].reshape(1, c),
                        rw_r_k[l].reshape(1, c), e128)

        x = _mix_out(att, rw, w_out[l].astype(BF16), x, mod[l], 2, grp)

        wr_pad = jnp.zeros((d, LANES), F32).at[:, :N_EXPERTS].set(w_router[l])
        br_pad = jnp.full((1, LANES), NEG_BIG, F32).at[0, :N_EXPERTS].set(b_router[l])
        x = _moe_layer(x, norm_ffn[l], mod[l], wr_pad, br_pad, w_mlp1, b_mlp1, w_mlp2, b_mlp2, l, grp)

        k_out.append(qkv[:, c:2 * c])
        v_out.append(qkv[:, 2 * c:3 * c])
        shift_p.append(pr[tp - 1:np_rows:tp, :C_SHIFT])
        shift_s.append(pr[np_rows + ts - 1::ts, :C_SHIFT])
        wkv_p.append(_unpack_hbd(h_p.reshape(bp * N_PAIRS, LANES, LANES), bp))
        wkv_s.append(_unpack_state(st_s))

    y = _final_norm(x, norm_final)
    k_all, v_all = jnp.stack(k_out), jnp.stack(v_out)
    pages = tp // PAGE_SIZE
    return (y[:np_rows].reshape(bp, tp, d), y[np_rows:].reshape(bs, ts, d),
            k_all[:, :np_rows].reshape(depth, bp, pages, PAGE_SIZE, N_HEADS, HEAD_DIM),
            v_all[:, :np_rows].reshape(depth, bp, pages, PAGE_SIZE, N_HEADS, HEAD_DIM),
            jnp.stack(shift_p), jnp.stack(wkv_p),
            k_all[:, np_rows:].reshape(depth, bs, ts, N_HEADS, HEAD_DIM),
            v_all[:, np_rows:].reshape(depth, bs, ts, N_HEADS, HEAD_DIM),
            jnp.stack(shift_s), jnp.stack(wkv_s))
```
